```python
import jax
import jax.numpy as jnp
from jax import lax
import numpy as np

D_MODEL = 1024
BATCH = 2
SEQ = 16384
DEPTH = 1

GRID_W = 64
CTX_LEN = 256
HEAD_DIM = 64
A_HEADS = 8
A_KV_HEADS = 2
A_GROUPS = A_HEADS // A_KV_HEADS
B_HEADS = 8
B_KV_HEADS = 2
B_GROUPS = B_HEADS // B_KV_HEADS
WINDOW = 128
BLOCK = 128
ROPE_THETA = 10000.0
D_FF = 2816
CONV_WIDTH = 3
LN_EPS = 1e-5
QK_EPS = 1e-6
N_MOD = 6
DEEPNORM_ALPHA = (2.0 * DEPTH) ** 0.25
DEEPNORM_BETA = (8.0 * DEPTH) ** -0.25

OFF_QA = 0
OFF_KA = OFF_QA + A_HEADS * HEAD_DIM
OFF_VA = OFF_KA + A_KV_HEADS * HEAD_DIM
OFF_QB = OFF_VA + A_KV_HEADS * HEAD_DIM
OFF_KB = OFF_QB + B_HEADS * HEAD_DIM
OFF_VB = OFF_KB + B_KV_HEADS * HEAD_DIM
OFF_GA = OFF_VB + B_KV_HEADS * HEAD_DIM
OFF_GB = OFF_GA + D_MODEL
IN_COLS = OFF_GB + D_MODEL

kernel_name = "hybrid_window_axial_gqa_convffn_dit_layer"


def _layer_norm(x, g, b):
    xf = x.astype(jnp.float32)
    mu = jnp.mean(xf, axis=-1, keepdims=True)
    var = jnp.mean(jnp.square(xf - mu), axis=-1, keepdims=True)
    return ((xf - mu) * lax.rsqrt(var + LN_EPS) * g + b).astype(x.dtype)


def _qk_rms(t, g):
    tf = t.astype(jnp.float32)
    return (tf * lax.rsqrt(jnp.mean(tf * tf, axis=-1, keepdims=True) + QK_EPS) * g).astype(t.dtype)


def _axial_rope_tables(n_tok, dtype):
    pos = jnp.arange(n_tok, dtype=jnp.int32)
    rows = (pos // GRID_W).astype(jnp.float32)
    cols = (pos % GRID_W).astype(jnp.float32)
    n_freq = HEAD_DIM // 4
    inv_freq = ROPE_THETA ** (-jnp.arange(n_freq, dtype=jnp.float32) / n_freq)
    ang_r = rows[:, None, None] * inv_freq
    ang_c = cols[:, None, None] * inv_freq
    return (jnp.cos(ang_r).astype(dtype), jnp.sin(ang_r).astype(dtype),
            jnp.cos(ang_c).astype(dtype), jnp.sin(ang_c).astype(dtype))


def _rotate_half(t, cos, sin):
    t1, t2 = jnp.split(t, 2, axis=-1)
    return jnp.concatenate([t1 * cos - t2 * sin, t2 * cos + t1 * sin], axis=-1)


def _rope_2d(t, tables):
    cos_r, sin_r, cos_c, sin_c = tables
    t_row, t_col = jnp.split(t, 2, axis=-1)
    return jnp.concatenate([_rotate_half(t_row, cos_r, sin_r), _rotate_half(t_col, cos_c, sin_c)], axis=-1)


def _gqa_scores(q, k):
    return jnp.einsum("bqhgd,bkhd->bhgqk", q, k).astype(jnp.float32) * (HEAD_DIM ** -0.5)


def _gqa_values(p, v):
    return jnp.einsum("bhgqk,bkhd->bqhgd", p.astype(v.dtype), v)


def _sink_softmax(scores, sink):
    sink = sink.astype(jnp.float32)[:, :, None, None]
    m = jnp.maximum(jnp.max(scores, axis=-1, keepdims=True), sink)
    e = jnp.exp(scores - m)
    return e / (jnp.sum(e, axis=-1, keepdims=True) + jnp.exp(sink - m))


def _windowed_attention(q, k, v, k_ctx, v_ctx, sink):
    bsz, n_tok = q.shape[:2]
    pad = ((0, 0), (BLOCK, BLOCK), (0, 0), (0, 0))
    k_pad = jnp.pad(k, pad)
    v_pad = jnp.pad(v, pad)
    ctx_mask = jnp.ones((BLOCK, k_ctx.shape[1]), dtype=bool)

    def one_block(i):
        start = i * BLOCK
        q_blk = lax.dynamic_slice_in_dim(q, start, BLOCK, axis=1)
        k_blk = jnp.concatenate([lax.dynamic_slice_in_dim(k_pad, start, 3 * BLOCK, axis=1), k_ctx], axis=1)
        v_blk = jnp.concatenate([lax.dynamic_slice_in_dim(v_pad, start, 3 * BLOCK, axis=1), v_ctx], axis=1)
        q_pos = start + jnp.arange(BLOCK)
        k_pos = start - BLOCK + jnp.arange(3 * BLOCK)
        band = ((jnp.abs(q_pos[:, None] - k_pos[None, :]) <= WINDOW)
                & (k_pos[None, :] >= 0) & (k_pos[None, :] < n_tok))
        mask = jnp.concatenate([band, ctx_mask], axis=1)
        scores = jnp.where(mask, _gqa_scores(q_blk, k_blk), -jnp.inf)
        return _gqa_values(_sink_softmax(scores, sink), v_blk)

    out = lax.map(one_block, jnp.arange(n_tok // BLOCK))
    return jnp.moveaxis(out, 0, 1).reshape(bsz, n_tok, -1)


def _global_attention(q, k_all, v_all):
    bsz, n_tok = q.shape[:2]

    def one_block(i):
        q_blk = lax.dynamic_slice_in_dim(q, i * BLOCK, BLOCK, axis=1)
        p = jax.nn.softmax(_gqa_scores(q_blk, k_all), axis=-1)
        return _gqa_values(p, v_all)

    out = lax.map(one_block, jnp.arange(n_tok // BLOCK))
    return jnp.moveaxis(out, 0, 1).reshape(bsz, n_tok, -1)


def _merge_branches(o_a, o_b, gate_logits, w_branch_a, w_branch_b, w_out):
    g_a, g_b = jnp.split(jax.nn.sigmoid(gate_logits), 2, axis=-1)
    return (g_a * (o_a @ w_branch_a) + g_b * (o_b @ w_branch_b)) @ w_out


def _context_kv(h_c, w_in, b_in, k_norm_g):
    bsz, n_ctx, _ = h_c.shape
    kv_a = (h_c @ w_in[:, OFF_KA:OFF_QB] + b_in[OFF_KA:OFF_QB]).reshape(bsz, n_ctx, 2 * A_KV_HEADS, HEAD_DIM)
    kv_b = (h_c @ w_in[:, OFF_KB:OFF_GA] + b_in[OFF_KB:OFF_GA]).reshape(bsz, n_ctx, 2 * B_KV_HEADS, HEAD_DIM)
    k_a, v_a = jnp.split(kv_a, 2, axis=2)
    k_b, v_b = jnp.split(kv_b, 2, axis=2)
    return (k_a, v_a, _qk_rms(k_b, k_norm_g), v_b)


def _latent_token_mixer(h, kv_ctx, w_in, b_in, sink, q_norm_g, k_norm_g,
                        w_branch_a, w_branch_b, w_out, rope):
    bsz, n_tok, _ = h.shape
    k_a_c, v_a_c, k_b_c, v_b_c = kv_ctx
    proj = h @ w_in + b_in

    def heads(lo, hi, n):
        return proj[..., lo:hi].reshape(bsz, n_tok, n, HEAD_DIM)

    q_a = _rope_2d(heads(OFF_QA, OFF_KA, A_HEADS), rope).reshape(bsz, n_tok, A_KV_HEADS, A_GROUPS, HEAD_DIM)
    k_a = _rope_2d(heads(OFF_KA, OFF_VA, A_KV_HEADS), rope)
    v_a = heads(OFF_VA, OFF_QB, A_KV_HEADS)
    q_b = _rope_2d(_qk_rms(heads(OFF_QB, OFF_KB, B_HEADS), q_norm_g), rope).reshape(
        bsz, n_tok, B_KV_HEADS, B_GROUPS, HEAD_DIM)
    k_b = _rope_2d(_qk_rms(heads(OFF_KB, OFF_VB, B_KV_HEADS), k_norm_g), rope)
    v_b = heads(OFF_VB, OFF_GA, B_KV_HEADS)
    o_a = _windowed_attention(q_a, k_a, v_a, k_a_c, v_a_c, sink.reshape(A_KV_HEADS, A_GROUPS))
    o_b = _global_attention(q_b, jnp.concatenate([k_b, k_b_c], axis=1), jnp.concatenate([v_b, v_b_c], axis=1))
    return _merge_branches(o_a, o_b, proj[..., OFF_GA:], w_branch_a, w_branch_b, w_out)


def _context_token_mixer(h_c, kv_ctx, w_in, b_in, sink, q_norm_g, w_branch_a, w_branch_b, w_out):
    bsz, n_ctx, _ = h_c.shape
    k_a_c, v_a_c, k_b_c, v_b_c = kv_ctx
    q_a = (h_c @ w_in[:, OFF_QA:OFF_KA] + b_in[OFF_QA:OFF_KA]).reshape(bsz, n_ctx, A_KV_HEADS, A_GROUPS, HEAD_DIM)
    q_b = _qk_rms((h_c @ w_in[:, OFF_QB:OFF_KB] + b_in[OFF_QB:OFF_KB]).reshape(bsz, n_ctx, B_HEADS, HEAD_DIM),
                  q_norm_g).reshape(bsz, n_ctx, B_KV_HEADS, B_GROUPS, HEAD_DIM)
    gate_logits = h_c @ w_in[:, OFF_GA:] + b_in[OFF_GA:]
    o_a = _gqa_values(_sink_softmax(_gqa_scores(q_a, k_a_c), sink.reshape(A_KV_HEADS, A_GROUPS)), v_a_c)
    o_b = _gqa_values(jax.nn.softmax(_gqa_scores(q_b, k_b_c), axis=-1), v_b_c)
    return _merge_branches(o_a.reshape(bsz, n_ctx, -1), o_b.reshape(bsz, n_ctx, -1), gate_logits,
                           w_branch_a, w_branch_b, w_out)


def _conv_ffn(h, w_up, conv_w, conv_b, w_down):
    u = h @ w_up
    half = CONV_WIDTH // 2
    u = lax.conv_general_dilated(u, conv_w[:, None, :], window_strides=(1,), padding=((half, half),),
                                 dimension_numbers=("NWC", "WIO", "NWC"),
                                 feature_group_count=u.shape[-1]) + conv_b
    gate, val = jnp.split(u, 2, axis=-1)
    return (jax.nn.silu(gate) * val) @ w_down


def setup_inputs(seed: int = 0) -> dict:
    key = jax.random.key(seed)
    ks = jax.random.split(key, 24)
    f32 = jnp.float32

    def normal(k, shape, scale):
        return jax.random.normal(k, shape, f32) * scale

    L = DEPTH
    return {
        "x": normal(ks[0], (BATCH, SEQ, D_MODEL), 1.0),
        "c": normal(ks[1], (BATCH, D_MODEL), 1.0),
        "ctx": normal(ks[2], (BATCH, CTX_LEN, D_MODEL), 1.0),
        "c_ctx": normal(ks[3], (D_MODEL,), 1.0),
        "w_mod": normal(ks[4], (L, D_MODEL, N_MOD * D_MODEL), 0.5 * D_MODEL ** -0.5),
        "b_mod": normal(ks[5], (L, N_MOD * D_MODEL), 0.02),
        "w_in": normal(ks[6], (L, D_MODEL, IN_COLS), D_MODEL ** -0.5),
        "b_in": normal(ks[7], (L, IN_COLS), 0.02),
        "attn_sink": normal(ks[8], (L, A_HEADS), 0.5),
        "q_norm_g": 1.0 + normal(ks[9], (L, HEAD_DIM), 0.05),
        "k_norm_g": 1.0 + normal(ks[10], (L, HEAD_DIM), 0.05),
        "w_branch_a": normal(ks[11], (L, A_HEADS * HEAD_DIM, D_MODEL), (A_HEADS * HEAD_DIM) ** -0.5),
        "w_branch_b": normal(ks[12], (L, B_HEADS * HEAD_DIM, D_MODEL), (B_HEADS * HEAD_DIM) ** -0.5),
        "w_out": normal(ks[13], (L, D_MODEL, D_MODEL), DEEPNORM_BETA * D_MODEL ** -0.5),
        "ln1_g": 1.0 + normal(ks[14], (L, D_MODEL), 0.05),
        "ln1_b": normal(ks[15], (L, D_MODEL), 0.02),
        "w_up": normal(ks[16], (L, D_MODEL, 2 * D_FF), D_MODEL ** -0.5),
        "conv_w": normal(ks[17], (L, CONV_WIDTH, 2 * D_FF), CONV_WIDTH ** -0.5),
        "conv_b": normal(ks[18], (L, 2 * D_FF), 0.02),
        "w_down": normal(ks[19], (L, D_FF, D_MODEL), DEEPNORM_BETA * D_FF ** -0.5),
        "ln2_g": 1.0 + normal(ks[20], (L, D_MODEL), 0.05),
        "ln2_b": normal(ks[21], (L, D_MODEL), 0.02),
    }


def reference(x, c, ctx, c_ctx, w_mod, b_mod, w_in, b_in, attn_sink, q_norm_g, k_norm_g,
              w_branch_a, w_branch_b, w_out, ln1_g, ln1_b, w_up, conv_w, conv_b, w_down,
              ln2_g, ln2_b):
    n_tok = x.shape[1]
    rope = _axial_rope_tables(n_tok, x.dtype)
    for l in range(DEPTH):
        last = l == DEPTH - 1
        mod = jax.nn.silu(c) @ w_mod[l] + b_mod[l]
        shift1, scale1, gate1, shift2, scale2, gate2 = jnp.split(mod[:, None, :], N_MOD, axis=-1)
        n_mod_c = 2 if last else N_MOD
        mod_c = jax.nn.silu(c_ctx) @ w_mod[l, :, :n_mod_c * D_MODEL] + b_mod[l, :n_mod_c * D_MODEL]
        mods_c = jnp.split(mod_c, n_mod_c)
        h_c = ctx * (1.0 + mods_c[1]) + mods_c[0]
        kv_ctx = _context_kv(h_c, w_in[l], b_in[l], k_norm_g[l])

        h = x * (1.0 + scale1) + shift1
        y = _latent_token_mixer(h, kv_ctx, w_in[l], b_in[l], attn_sink[l], q_norm_g[l], k_norm_g[l],
                                w_branch_a[l], w_branch_b[l], w_out[l], rope)
        x = _layer_norm(DEEPNORM_ALPHA * x + gate1 * y, ln1_g[l], ln1_b[l])

        h = x * (1.0 + scale2) + shift2
        y = _conv_ffn(h, w_up[l], conv_w[l], conv_b[l], w_down[l])
        x = _layer_norm(DEEPNORM_ALPHA * x + gate2 * y, ln2_g[l], ln2_b[l])

        if not last:
            y_c = _context_token_mixer(h_c, kv_ctx, w_in[l], b_in[l], attn_sink[l], q_norm_g[l],
                                       w_branch_a[l], w_branch_b[l], w_out[l])
            ctx = _layer_norm(DEEPNORM_ALPHA * ctx + mods_c[2] * y_c, ln1_g[l], ln1_b[l])
            h_c = ctx * (1.0 + mods_c[4]) + mods_c[3]
            y_c = _conv_ffn(h_c, w_up[l], conv_w[l], conv_b[l], w_down[l])
            ctx = _layer_norm(DEEPNORM_ALPHA * ctx + mods_c[5] * y_c, ln2_g[l], ln2_b[l])
    return x
```

```python
import functools

import jax
import jax.numpy as jnp
from jax import lax
from jax.experimental import pallas as pl
from jax.experimental.pallas import tpu as pltpu

F32 = jnp.float32
BF16 = jnp.bfloat16

GRID_W = 64
HEAD_DIM = 64
N_HEADS = 8
N_KV_HEADS = 2
GROUPS = N_HEADS // N_KV_HEADS
WINDOW = 128
ROPE_THETA = 10000.0
LN_EPS = 1e-5
QK_EPS = 1e-6
N_MOD = 6
DEPTH = 1
DEEPNORM_ALPHA = (2.0 * DEPTH) ** 0.25
CONV_WIDTH = 3

LANES = 128
SUBLANES = 8
VMEM_LIMIT_BYTES = 56 * 1024 * 1024

ROW_TILE = 512
Q_TILE = 256
K_TILE = 512
V_ROWS = HEAD_DIM + 16
FF_CHUNK = 256
HALO = SUBLANES


def _params(n_grid):
    return pltpu.CompilerParams(dimension_semantics=("arbitrary",) * n_grid,
                                vmem_limit_bytes=VMEM_LIMIT_BYTES)


def _resident(shape):
    zeros = (0,) * len(shape)
    return pl.BlockSpec(shape, lambda *_: zeros, pipeline_mode=pl.Buffered(1))


def _mod_kernel(c_ref, w_ref, b_ref, o_ref):
    c = c_ref[...]
    a = (c * jax.nn.sigmoid(c)).astype(BF16)
    o_ref[...] = jnp.dot(a, w_ref[...], preferred_element_type=F32) + b_ref[...]


def _modulation(c_rows, w_mod, b_mod):
    rows, d = c_rows.shape
    n = w_mod.shape[1]
    tn = 1024
    return pl.pallas_call(
        _mod_kernel,
        grid=(n // tn,),
        in_specs=[pl.BlockSpec((rows, d), lambda j: (0, 0)),
                  pl.BlockSpec((d, tn), lambda j: (0, j)),
                  pl.BlockSpec((1, tn), lambda j: (0, j))],
        out_specs=pl.BlockSpec((rows, tn), lambda j: (0, j)),
        out_shape=jax.ShapeDtypeStruct((rows, n), F32),
        compiler_params=_params(1),
        name="modulation",
    )(c_rows, w_mod, b_mod)


def _head_mean_matrix():
    r = lax.broadcasted_iota(jnp.int32, (LANES, LANES), 0) // HEAD_DIM
    c = lax.broadcasted_iota(jnp.int32, (LANES, LANES), 1) // HEAD_DIM
    return jnp.where(r == c, 1.0 / HEAD_DIM, 0.0).astype(BF16)


def _rms_heads(t, gain, gmat):
    ms = jnp.dot((t * t).astype(BF16), gmat, preferred_element_type=F32)
    return t * lax.rsqrt(ms + QK_EPS) * gain


def _rope(t, cos, sin_signed, first_half):
    partner = jnp.where(first_half, pltpu.roll(t, LANES - 16, 1), pltpu.roll(t, 16, 1))
    return t * cos + partner * sin_signed


def _store_vt(vt_ref, v, tile_major):
    vt = v.T.astype(BF16)
    ones = jnp.ones((V_ROWS - HEAD_DIM, v.shape[0]), BF16)
    for kvh in range(N_KV_HEADS):
        blk = vt[kvh * HEAD_DIM:(kvh + 1) * HEAD_DIM]
        if tile_major:
            vt_ref[0, kvh, 0, 0:HEAD_DIM, :] = blk
            vt_ref[0, kvh, 0, HEAD_DIM:V_ROWS, :] = ones
        else:
            vt_ref[0, kvh, 0:HEAD_DIM, :] = blk
            vt_ref[0, kvh, HEAD_DIM:V_ROWS, :] = ones


def _latent_proj_kernel(x_ref, scale_ref, shift_ref, w_ref, b_ref, cos_ref, sin_ref, gq_ref, gk_ref,
                        qa_ref, ka_ref, va_ref, qb_ref, kb_ref, vb_ref, gate_ref, *, offs):
    off_qa, off_ka, off_qb, off_kb, off_g, n_cols = offs
    h = (x_ref[0] * (1.0 + scale_ref[0]) + shift_ref[0]).astype(BF16)
    cos = cos_ref[...]
    sin = sin_ref[...]
    lane = lax.broadcasted_iota(jnp.int32, cos.shape, 1)
    first_half = (lane & 31) < 16
    gmat = _head_mean_matrix()
    q_scale = HEAD_DIM ** -0.5

    def proj(lo, n):
        return jnp.dot(h, w_ref[:, lo:lo + n], preferred_element_type=F32) + b_ref[:, lo:lo + n]

    n_q = N_HEADS * HEAD_DIM
    t = proj(off_qa, n_q)
    for j in range(n_q // LANES):
        s = _rope(t[:, j * LANES:(j + 1) * LANES], cos, sin, first_half) * q_scale
        qa_ref[0, j * LANES:(j + 1) * LANES, :] = s.T.astype(BF16)
    t = proj(off_ka, 2 * LANES)
    ka_ref[0] = _rope(t[:, :LANES], cos, sin, first_half).astype(BF16)
    _store_vt(va_ref, t[:, LANES:], tile_major=False)
    t = proj(off_qb, n_q)
    for j in range(n_q // LANES):
        s = _rms_heads(t[:, j * LANES:(j + 1) * LANES], gq_ref[...], gmat)
        s = _rope(s, cos, sin, first_half) * q_scale
        qb_ref[0, j * LANES:(j + 1) * LANES, :] = s.T.astype(BF16)
    t = proj(off_kb, 2 * LANES)
    s = _rms_heads(t[:, :LANES], gk_ref[...], gmat)
    kb_ref[0] = _rope(s, cos, sin, first_half).astype(BF16)
    _store_vt(vb_ref, t[:, LANES:], tile_major=True)
    n_gate = n_cols - off_g
    step = 512
    for j in range(n_gate // step):
        g = proj(off_g + j * step, step)
        gate_ref[0, :, j * step:(j + 1) * step] = jax.nn.sigmoid(g).astype(BF16)


def _context_proj_kernel(x_ref, scale_ref, shift_ref, w_ref, b_ref, gk_ref,
                         ka_ref, va_ref, kb_ref, vb_ref):
    h = (x_ref[0] * (1.0 + scale_ref[0]) + shift_ref[0]).astype(BF16)
    gmat = _head_mean_matrix()
    t = jnp.dot(h, w_ref[...], preferred_element_type=F32) + b_ref[...]
    ka_ref[0] = t[:, 0:LANES].astype(BF16)
    _store_vt(va_ref, t[:, LANES:2 * LANES], tile_major=False)
    kb_ref[0] = _rms_heads(t[:, 2 * LANES:3 * LANES], gk_ref[...], gmat).astype(BF16)
    _store_vt(vb_ref, t[:, 3 * LANES:4 * LANES], tile_major=False)


def _latent_projection(x, scale, shift, w_in, b_in, cos, sin, gq, gk, offs):
    bsz, n_tok, d = x.shape
    n_cols = w_in.shape[1]
    tm = ROW_TILE
    n_q = N_HEADS * HEAD_DIM
    vec = lambda: pl.BlockSpec((1, 1, d), lambda b, i: (b, 0, 0))
    out_shape = (
        jax.ShapeDtypeStruct((bsz, n_q, n_tok), BF16),
        jax.ShapeDtypeStruct((bsz, n_tok, LANES), BF16),
        jax.ShapeDtypeStruct((bsz, N_KV_HEADS, V_ROWS, n_tok), BF16),
        jax.ShapeDtypeStruct((bsz, n_q, n_tok), BF16),
        jax.ShapeDtypeStruct((bsz, n_tok, LANES), BF16),
        jax.ShapeDtypeStruct((bsz, N_KV_HEADS, n_tok // K_TILE, V_ROWS, K_TILE), BF16),
        jax.ShapeDtypeStruct((bsz, n_tok, n_cols - offs[4]), BF16),
    )
    out_specs = (
        pl.BlockSpec((1, n_q, tm), lambda b, i: (b, 0, i)),
        pl.BlockSpec((1, tm, LANES), lambda b, i: (b, i, 0)),
        pl.BlockSpec((1, N_KV_HEADS, V_ROWS, tm), lambda b, i: (b, 0, 0, i)),
        pl.BlockSpec((1, n_q, tm), lambda b, i: (b, 0, i)),
        pl.BlockSpec((1, tm, LANES), lambda b, i: (b, i, 0)),
        pl.BlockSpec((1, N_KV_HEADS, 1, V_ROWS, K_TILE), lambda b, i: (b, 0, i, 0, 0)),
        pl.BlockSpec((1, tm, n_cols - offs[4]), lambda b, i: (b, i, 0)),
    )
    return pl.pallas_call(
        functools.partial(_latent_proj_kernel, offs=offs + (n_cols,)),
        grid=(bsz, n_tok // tm),
        in_specs=[pl.BlockSpec((1, tm, d), lambda b, i: (b, i, 0)), vec(), vec(),
                  _resident((d, n_cols)), _resident((1, n_cols)),
                  pl.BlockSpec((tm, LANES), lambda b, i: (i, 0)),
                  pl.BlockSpec((tm, LANES), lambda b, i: (i, 0)),
                  _resident((1, LANES)), _resident((1, LANES))],
        out_specs=out_specs,
        out_shape=out_shape,
        compiler_params=_params(2),
        name="latent_projection",
    )(x, scale, shift, w_in, b_in, cos, sin, gq, gk)


def _context_projection(ctx, scale, shift, w_kv, b_kv, gk):
    bsz, n_ctx, d = ctx.shape
    n = w_kv.shape[1]
    vec = lambda: pl.BlockSpec((1, 1, d), lambda b: (0, 0, 0))
    kspec = lambda: pl.BlockSpec((1, n_ctx, LANES), lambda b: (b, 0, 0))
    vspec = lambda: pl.BlockSpec((1, N_KV_HEADS, V_ROWS, n_ctx), lambda b: (b, 0, 0, 0))
    kshape = jax.ShapeDtypeStruct((bsz, n_ctx, LANES), BF16)
    vshape = jax.ShapeDtypeStruct((bsz, N_KV_HEADS, V_ROWS, n_ctx), BF16)
    return pl.pallas_call(
        _context_proj_kernel,
        grid=(bsz,),
        in_specs=[pl.BlockSpec((1, n_ctx, d), lambda b: (b, 0, 0)), vec(), vec(),
                  _resident((d, n)), _resident((1, n)), _resident((1, LANES))],
        out_specs=(kspec(), vspec(), kspec(), vspec()),
        out_shape=(kshape, vshape, kshape, vshape),
        compiler_params=_params(1),
        name="context_projection",
    )(ctx, scale, shift, w_kv, b_kv, gk)


def _expand_queries(qx_ref, qt_ref):
    zeros = jnp.zeros((HEAD_DIM, qt_ref.shape[2]), BF16)
    for hh in range(N_HEADS):
        q = qt_ref[0, hh * HEAD_DIM:(hh + 1) * HEAD_DIM, :]
        if hh // GROUPS == 0:
            qx_ref[hh] = jnp.concatenate([q, zeros], axis=0)
        else:
            qx_ref[hh] = jnp.concatenate([zeros, q], axis=0)


def _window_attn_kernel(sink_ref, qt_ref, kp_ref, kc_ref, kn_ref, kx_ref,
                        vp_ref, vc_ref, vn_ref, vx_ref, o_ref, qx_ref, ot_ref, *, n_tok):
    i = pl.program_id(1)
    tq = qt_ref.shape[2]
    _expand_queries(qx_ref, qt_ref)
    k_all = jnp.concatenate([kp_ref[0], kc_ref[0], kn_ref[0], kx_ref[0]], axis=0)
    n_lat = tq + 2 * WINDOW
    n_keys = k_all.shape[0]
    kk = lax.broadcasted_iota(jnp.int32, (n_keys, tq), 0)
    qq = lax.broadcasted_iota(jnp.int32, (n_keys, tq), 1)
    k_pos = i * tq - WINDOW + kk
    rel = kk - WINDOW - qq
    ok = (kk >= n_lat) | ((jnp.abs(rel) <= WINDOW) & (k_pos >= 0) & (k_pos < n_tok))
    bias = jnp.where(ok, 0.0, -jnp.inf).astype(F32)
    for hh in range(N_HEADS):
        kvh = hh // GROUPS
        vt = jnp.concatenate([vp_ref[0, kvh], vc_ref[0, kvh], vn_ref[0, kvh], vx_ref[0, kvh]], axis=1)
        s = jnp.dot(k_all, qx_ref[hh], preferred_element_type=F32) + bias
        sink = sink_ref[hh]
        m = jnp.maximum(jnp.max(s, axis=0, keepdims=True), sink)
        p = jnp.exp(s - m).astype(BF16)
        o = jnp.dot(vt, p, preferred_element_type=F32)
        denom = o[HEAD_DIM:HEAD_DIM + 1] + jnp.exp(sink - m)
        ot_ref[hh * HEAD_DIM:(hh + 1) * HEAD_DIM, :] = o[0:HEAD_DIM] / denom
    o_ref[0] = ot_ref[...].T.astype(BF16)


def _window_attention(sink, qt, k, vt, k_ctx, vt_ctx):
    bsz, n_q, n_tok = qt.shape
    n_ctx = k_ctx.shape[1]
    tq = Q_TILE
    r = tq // WINDOW
    n_wblk = n_tok // WINDOW
    prev = lambda i: jnp.maximum(i * r - 1, 0)
    nxt = lambda i: jnp.minimum((i + 1) * r, n_wblk - 1)
    return pl.pallas_call(
        functools.partial(_window_attn_kernel, n_tok=n_tok),
        grid=(bsz, n_tok // tq),
        in_specs=[pl.BlockSpec(memory_space=pltpu.SMEM),
                  pl.BlockSpec((1, n_q, tq), lambda b, i: (b, 0, i)),
                  pl.BlockSpec((1, WINDOW, LANES), lambda b, i: (b, prev(i), 0)),
                  pl.BlockSpec((1, tq, LANES), lambda b, i: (b, i, 0)),
                  pl.BlockSpec((1, WINDOW, LANES), lambda b, i: (b, nxt(i), 0)),
                  pl.BlockSpec((1, n_ctx, LANES), lambda b, i: (b, 0, 0)),
                  pl.BlockSpec((1, N_KV_HEADS, V_ROWS, WINDOW), lambda b, i: (b, 0, 0, prev(i))),
                  pl.BlockSpec((1, N_KV_HEADS, V_ROWS, tq), lambda b, i: (b, 0, 0, i)),
                  pl.BlockSpec((1, N_KV_HEADS, V_ROWS, WINDOW), lambda b, i: (b, 0, 0, nxt(i))),
                  pl.BlockSpec((1, N_KV_HEADS, V_ROWS, n_ctx), lambda b, i: (b, 0, 0, 0))],
        out_specs=pl.BlockSpec((1, tq, n_q), lambda b, i: (b, i, 0)),
        out_shape=jax.ShapeDtypeStruct((bsz, n_tok, n_q), BF16),
        scratch_shapes=[pltpu.VMEM((N_HEADS, LANES, tq), BF16),
                        pltpu.VMEM((n_q, tq), F32)],
        compiler_params=_params(2),
        name="window_attention",
    )(sink, qt, k, k, k, k_ctx, vt, vt, vt, vt_ctx)


def _global_attn_kernel(qt_ref, k_ref, vt_ref, kx_ref, vx_ref, o_ref, qx_ref, m_ref, acc_ref, ot_ref):
    tq = qt_ref.shape[2]
    n_tiles = vt_ref.shape[2]
    tk = vt_ref.shape[4]
    _expand_queries(qx_ref, qt_ref)
    m_ref[...] = jnp.full(m_ref.shape, -jnp.inf, F32)
    acc_ref[...] = jnp.zeros(acc_ref.shape, F32)

    def key_tile(k, vts):
        for hh in range(N_HEADS):
            s = jnp.dot(k, qx_ref[hh], preferred_element_type=F32)
            m_old = m_ref[hh]
            m_new = jnp.maximum(m_old, jnp.max(s, axis=0, keepdims=True))
            p = jnp.exp(s - m_new).astype(BF16)
            pv = jnp.dot(vts[hh // GROUPS], p, preferred_element_type=F32)
            acc_ref[hh] = acc_ref[hh] * jnp.exp(m_old - m_new) + pv
            m_ref[hh] = m_new

    def body(t, carry):
        k = k_ref[0, pl.ds(pl.multiple_of(t * tk, tk), tk), :]
        key_tile(k, [vt_ref[0, kvh, t] for kvh in range(N_KV_HEADS)])
        return carry

    lax.fori_loop(0, n_tiles, body, 0)
    key_tile(kx_ref[0], [vx_ref[0, kvh] for kvh in range(N_KV_HEADS)])
    for hh in range(N_HEADS):
        acc = acc_ref[hh]
        ot_ref[hh * HEAD_DIM:(hh + 1) * HEAD_DIM, :] = acc[0:HEAD_DIM] / acc[HEAD_DIM:HEAD_DIM + 1]
    o_ref[0] = ot_ref[...].T.astype(BF16)


def _global_attention(qt, k, vt, k_ctx, vt_ctx):
    bsz, n_q, n_tok = qt.shape
    n_ctx = k_ctx.shape[1]
    tq = Q_TILE
    n_tiles, tk = vt.shape[2], vt.shape[4]
    return pl.pallas_call(
        _global_attn_kernel,
        grid=(bsz, n_tok // tq),
        in_specs=[pl.BlockSpec((1, n_q, tq), lambda b, i: (b, 0, i)),
                  pl.BlockSpec((1, n_tok, LANES), lambda b, i: (b, 0, 0)),
                  pl.BlockSpec((1, N_KV_HEADS, n_tiles, V_ROWS, tk), lambda b, i: (b, 0, 0, 0, 0)),
                  pl.BlockSpec((1, n_ctx, LANES), lambda b, i: (b, 0, 0)),
                  pl.BlockSpec((1, N_KV_HEADS, V_ROWS, n_ctx), lambda b, i: (b, 0, 0, 0))],
        out_specs=pl.BlockSpec((1, tq, n_q), lambda b, i: (b, i, 0)),
        out_shape=jax.ShapeDtypeStruct((bsz, n_tok, n_q), BF16),
        scratch_shapes=[pltpu.VMEM((N_HEADS, LANES, tq), BF16),
                        pltpu.VMEM((N_HEADS, 1, tq), F32),
                        pltpu.VMEM((N_HEADS, V_ROWS, tq), F32),
                        pltpu.VMEM((n_q, tq), F32)],
        compiler_params=_params(2),
        name="global_attention",
    )(qt, k, vt, k_ctx, vt_ctx)


def _layer_norm(z, g, b):
    mu = jnp.mean(z, axis=-1, keepdims=True)
    zc = z - mu
    var = jnp.mean(zc * zc, axis=-1, keepdims=True)
    return zc * lax.rsqrt(var + LN_EPS) * g + b


def _merge_kernel(oa_ref, ob_ref, gate_ref, x_ref, g1_ref, wa_ref, wb_ref, wo_ref, lng_ref, lnb_ref, o_ref):
    d = x_ref.shape[2]
    a = jnp.dot(oa_ref[0], wa_ref[...], preferred_element_type=F32)
    b = jnp.dot(ob_ref[0], wb_ref[...], preferred_element_type=F32)
    g = gate_ref[0].astype(F32)
    mix = (g[:, :d] * a + g[:, d:] * b).astype(BF16)
    y = jnp.dot(mix, wo_ref[...], preferred_element_type=F32)
    z = DEEPNORM_ALPHA * x_ref[0] + g1_ref[0] * y
    o_ref[0] = _layer_norm(z, lng_ref[...], lnb_ref[...])


def _merge(o_a, o_b, gates, x, gate1, w_a, w_b, w_out, ln_g, ln_b):
    bsz, n_tok, d = x.shape
    n_q = o_a.shape[2]
    tm = ROW_TILE
    row = lambda n: pl.BlockSpec((1, tm, n), lambda b, i: (b, i, 0))
    return pl.pallas_call(
        _merge_kernel,
        grid=(bsz, n_tok // tm),
        in_specs=[row(n_q), row(n_q), row(2 * d), row(d),
                  pl.BlockSpec((1, 1, d), lambda b, i: (b, 0, 0)),
                  _resident((n_q, d)), _resident((n_q, d)), _resident((d, d)),
                  _resident((1, d)), _resident((1, d))],
        out_specs=row(d),
        out_shape=jax.ShapeDtypeStruct((bsz, n_tok, d), F32),
        compiler_params=_params(2),
        name="merge_ln1",
    )(o_a, o_b, gates, x, gate1, w_a, w_b, w_out, ln_g, ln_b)


def _conv_ffn_kernel(x_ref, xp_ref, xn_ref, scale_ref, shift_ref, g2_ref, wup_ref, cw_ref, cb_ref, wdn_ref,
                     lng_ref, lnb_ref, o_ref, acc_ref):
    i = pl.program_id(1)
    n_i = pl.num_programs(1)
    tm = x_ref.shape[1]
    x = x_ref[0]
    scale = 1.0 + scale_ref[0]
    shift = shift_ref[0]
    hp = jnp.where(i > 0, xp_ref[0] * scale + shift, 0.0)
    hn = jnp.where(i < n_i - 1, xn_ref[0] * scale + shift, 0.0)
    h = jnp.concatenate([hp, x * scale + shift, hn], axis=0).astype(BF16)
    acc_ref[...] = jnp.zeros(acc_ref.shape, F32)

    def chunk(c, carry):
        u = jnp.dot(h, wup_ref[c], preferred_element_type=F32)
        w = cw_ref[c]
        n = u.shape[0]
        u_prev = pltpu.roll(u, 1, 0)[HALO:HALO + tm]
        u_next = pltpu.roll(u, n - 1, 0)[HALO:HALO + tm]
        v = w[0:1] * u_prev + w[1:2] * u[HALO:HALO + tm] + w[2:3] * u_next + cb_ref[c]
        gate = v[:, :FF_CHUNK]
        val = v[:, FF_CHUNK:]
        act = (gate * jax.nn.sigmoid(gate) * val).astype(BF16)
        acc_ref[...] += jnp.dot(act, wdn_ref[c], preferred_element_type=F32)
        return carry

    lax.fori_loop(0, wup_ref.shape[0], chunk, 0)
    z = DEEPNORM_ALPHA * x + g2_ref[0] * acc_ref[...]
    o_ref[0] = _layer_norm(z, lng_ref[...], lnb_ref[...])


def _conv_ffn(x, scale, shift, gate2, w_up_c, conv_w_c, conv_b_c, w_down_c, ln_g, ln_b):
    bsz, n_tok, d = x.shape
    tm = ROW_TILE
    n_chunks = w_up_c.shape[0]
    hb = tm // HALO
    n_hblk = n_tok // HALO
    vec = lambda: pl.BlockSpec((1, 1, d), lambda b, i: (b, 0, 0))
    return pl.pallas_call(
        _conv_ffn_kernel,
        grid=(bsz, n_tok // tm),
        in_specs=[pl.BlockSpec((1, tm, d), lambda b, i: (b, i, 0)),
                  pl.BlockSpec((1, HALO, d), lambda b, i: (b, jnp.maximum(i * hb - 1, 0), 0)),
                  pl.BlockSpec((1, HALO, d), lambda b, i: (b, jnp.minimum((i + 1) * hb, n_hblk - 1), 0)),
                  vec(), vec(), vec(),
                  _resident((n_chunks, d, 2 * FF_CHUNK)),
                  _resident((n_chunks, CONV_WIDTH, 2 * FF_CHUNK)),
                  _resident((n_chunks, 1, 2 * FF_CHUNK)),
                  _resident((n_chunks, FF_CHUNK, d)),
                  _resident((1, d)), _resident((1, d))],
        out_specs=pl.BlockSpec((1, tm, d), lambda b, i: (b, i, 0)),
        out_shape=jax.ShapeDtypeStruct((bsz, n_tok, d), F32),
        scratch_shapes=[pltpu.VMEM((tm, d), F32)],
        compiler_params=_params(2),
        name="conv_ffn_ln2",
    )(x, x, x, scale, shift, gate2, w_up_c, conv_w_c, conv_b_c, w_down_c, ln_g, ln_b)


def _rope_tables(n_tok):
    pos = jnp.arange(n_tok, dtype=jnp.int32)
    rows = (pos // GRID_W).astype(F32)
    cols = (pos % GRID_W).astype(F32)
    n_freq = HEAD_DIM // 4
    inv_freq = ROPE_THETA ** (-jnp.arange(n_freq, dtype=F32) / n_freq)
    ang_r = rows[:, None] * inv_freq
    ang_c = cols[:, None] * inv_freq
    cos = jnp.concatenate([jnp.cos(ang_r)] * 2 + [jnp.cos(ang_c)] * 2, axis=-1)
    sin = jnp.concatenate([-jnp.sin(ang_r), jnp.sin(ang_r), -jnp.sin(ang_c), jnp.sin(ang_c)], axis=-1)
    reps = LANES // HEAD_DIM
    return jnp.tile(cos, (1, reps)), jnp.tile(sin, (1, reps))


def _chunk_cols(a, n_chunks):
    lead = a.shape[:-1]
    return jnp.moveaxis(a.reshape(lead + (n_chunks, FF_CHUNK)), -2, 0)


def kernel(x, c, ctx, c_ctx, w_mod, b_mod, w_in, b_in, attn_sink, q_norm_g, k_norm_g, w_branch_a, w_branch_b,
           w_out, ln1_g, ln1_b, w_up, conv_w, conv_b, w_down, ln2_g, ln2_b):
    bsz, n_tok, d = x.shape
    n_q = N_HEADS * HEAD_DIM
    n_kv = N_KV_HEADS * HEAD_DIM
    off_qa = 0
    off_ka = off_qa + n_q
    off_qb = off_ka + 2 * n_kv
    off_kb = off_qb + n_q
    off_g = off_kb + 2 * n_kv
    d_ff = w_down.shape[1]
    n_chunks = d_ff // FF_CHUNK
    assert n_tok % ROW_TILE == 0 and n_tok % K_TILE == 0 and n_tok % GRID_W == 0
    assert d_ff % FF_CHUNK == 0 and bsz <= SUBLANES - 1
    assert w_mod.shape[0] == DEPTH == 1 and ROW_TILE == K_TILE

    cos, sin = _rope_tables(n_tok)
    tile2 = lambda g: jnp.tile(g, LANES // HEAD_DIM)[None, :]
    c_rows = jnp.zeros((SUBLANES, d), F32).at[:bsz].set(c).at[bsz].set(c_ctx)
    mod = _modulation(c_rows, w_mod[0].astype(BF16), b_mod[0][None, :])
    shift1, scale1, gate1, shift2, scale2, gate2 = [mod[:bsz, None, j * d:(j + 1) * d] for j in range(N_MOD)]
    shift_c = mod[bsz:bsz + 1, None, 0:d]
    scale_c = mod[bsz:bsz + 1, None, d:2 * d]

    w_in_l = w_in[0].astype(BF16)
    b_in_l = b_in[0][None, :]
    gq, gk = tile2(q_norm_g[0]), tile2(k_norm_g[0])
    kv_cols = lambda a: jnp.concatenate([a[:, off_ka:off_qb], a[:, off_kb:off_g]], axis=1)
    kc_a, vc_a, kc_b, vc_b = _context_projection(ctx, scale_c, shift_c, kv_cols(w_in_l), kv_cols(b_in_l), gk)
    qa_t, k_a, va_t, qb_t, k_b, vb_t, gates = _latent_projection(
        x, scale1, shift1, w_in_l, b_in_l, cos, sin, gq, gk, (off_qa, off_ka, off_qb, off_kb, off_g))

    o_a = _window_attention(attn_sink[0], qa_t, k_a, va_t, kc_a, vc_a)
    o_b = _global_attention(qb_t, k_b, vb_t, kc_b, vc_b)

    x_mid = _merge(o_a, o_b, gates, x, gate1, w_branch_a[0].astype(BF16), w_branch_b[0].astype(BF16),
                   w_out[0].astype(BF16), ln1_g[0][None, :], ln1_b[0][None, :])

    pair = lambda a: jnp.concatenate([_chunk_cols(a[..., :d_ff], n_chunks), _chunk_cols(a[..., d_ff:], n_chunks)],
                                     axis=-1)
    w_down_c = w_down[0].astype(BF16).reshape(n_chunks, FF_CHUNK, d)
    return _conv_ffn(x_mid, scale2, shift2, gate2, pair(w_up[0].astype(BF16)), pair(conv_w[0]),
                     pair(conv_b[0][None, :]), w_down_c, ln2_g[0][None, :], ln2_b[0][None, :])
```

```python
import functools

import jax
import jax.numpy as jnp
from jax import lax
from jax.experimental import pallas as pl
from jax.experimental.pallas import tpu as pltpu

F32 = jnp.float32
BF16 = jnp.bfloat16

GRID_W = 64
HEAD_DIM = 64
N_HEADS = 8
N_KV_HEADS = 2
GROUPS = N_HEADS // N_KV_HEADS
WINDOW = 128
ROPE_THETA = 10000.0
LN_EPS = 1e-5
QK_EPS = 1e-6
N_MOD = 6
DEPTH = 1
DEEPNORM_ALPHA = (2.0 * DEPTH) ** 0.25
CONV_WIDTH = 3

LANES = 128
SUBLANES = 8
VMEM_LIMIT_BYTES = 56 * 1024 * 1024

ROW_TILE = 512
Q_TILE = 256
K_TILE = 512
V_ROWS = HEAD_DIM + 16
FF_CHUNK = 256
HALO = SUBLANES


def _params(n_grid):
    return pltpu.CompilerParams(dimension_semantics=("arbitrary",) * n_grid,
                                vmem_limit_bytes=VMEM_LIMIT_BYTES)


def _resident(shape):
    zeros = (0,) * len(shape)
    return pl.BlockSpec(shape, lambda *_: zeros, pipeline_mode=pl.Buffered(1))


def _mod_kernel(c_ref, w_ref, b_ref, o_ref):
    c = c_ref[...]
    a = (c * jax.nn.sigmoid(c)).astype(BF16)
    o_ref[...] = jnp.dot(a, w_ref[...], preferred_element_type=F32) + b_ref[...]


def _modulation(c_rows, w_mod, b_mod):
    rows, d = c_rows.shape
    n = w_mod.shape[1]
    tn = 1024
    return pl.pallas_call(
        _mod_kernel,
        grid=(n // tn,),
        in_specs=[pl.BlockSpec((rows, d), lambda j: (0, 0)),
                  pl.BlockSpec((d, tn), lambda j: (0, j)),
                  pl.BlockSpec((1, tn), lambda j: (0, j))],
        out_specs=pl.BlockSpec((rows, tn), lambda j: (0, j)),
        out_shape=jax.ShapeDtypeStruct((rows, n), F32),
        compiler_params=_params(1),
        name="modulation",
    )(c_rows, w_mod, b_mod)


def _head_mean_matrix():
    r = lax.broadcasted_iota(jnp.int32, (LANES, LANES), 0) // HEAD_DIM
    c = lax.broadcasted_iota(jnp.int32, (LANES, LANES), 1) // HEAD_DIM
    return jnp.where(r == c, 1.0 / HEAD_DIM, 0.0).astype(BF16)


def _rms_heads(t, gain, gmat):
    ms = jnp.dot((t * t).astype(BF16), gmat, preferred_element_type=F32)
    return t * lax.rsqrt(ms + QK_EPS) * gain


def _rope(t, cos, sin_signed, first_half):
    partner = jnp.where(first_half, pltpu.roll(t, LANES - 16, 1), pltpu.roll(t, 16, 1))
    return t * cos + partner * sin_signed


def _store_vt(vt_ref, v, tile_major):
    vt = v.T.astype(BF16)
    ones = jnp.ones((V_ROWS - HEAD_DIM, v.shape[0]), BF16)
    for kvh in range(N_KV_HEADS):
        blk = vt[kvh * HEAD_DIM:(kvh + 1) * HEAD_DIM]
        if tile_major:
            vt_ref[0, kvh, 0, 0:HEAD_DIM, :] = blk
            vt_ref[0, kvh, 0, HEAD_DIM:V_ROWS, :] = ones
        else:
            vt_ref[0, kvh, 0:HEAD_DIM, :] = blk
            vt_ref[0, kvh, HEAD_DIM:V_ROWS, :] = ones


def _latent_proj_kernel(x_ref, scale_ref, shift_ref, w_ref, b_ref, cos_ref, sin_ref, gq_ref, gk_ref,
                        qa_ref, ka_ref, va_ref, qb_ref, kb_ref, vb_ref, gate_ref, *, offs):
    off_qa, off_ka, off_qb, off_kb, off_g, n_cols = offs
    h = (x_ref[0] * (1.0 + scale_ref[0]) + shift_ref[0]).astype(BF16)
    cos = cos_ref[...]
    sin = sin_ref[...]
    lane = lax.broadcasted_iota(jnp.int32, cos.shape, 1)
    first_half = (lane & 31) < 16
    gmat = _head_mean_matrix()
    q_scale = HEAD_DIM ** -0.5

    def proj(lo, n):
        return jnp.dot(h, w_ref[:, lo:lo + n], preferred_element_type=F32) + b_ref[:, lo:lo + n]

    n_q = N_HEADS * HEAD_DIM
    t = proj(off_qa, n_q)
    for j in range(n_q // LANES):
        s = _rope(t[:, j * LANES:(j + 1) * LANES], cos, sin, first_half) * q_scale
        qa_ref[0, j * LANES:(j + 1) * LANES, :] = s.T.astype(BF16)
    t = proj(off_ka, 2 * LANES)
    ka_ref[0] = _rope(t[:, :LANES], cos, sin, first_half).astype(BF16)
    _store_vt(va_ref, t[:, LANES:], tile_major=False)
    t = proj(off_qb, n_q)
    for j in range(n_q // LANES):
        s = _rms_heads(t[:, j * LANES:(j + 1) * LANES], gq_ref[...], gmat)
        s = _rope(s, cos, sin, first_half) * q_scale
        qb_ref[0, j * LANES:(j + 1) * LANES, :] = s.T.astype(BF16)
    t = proj(off_kb, 2 * LANES)
    s = _rms_heads(t[:, :LANES], gk_ref[...], gmat)
    kb_ref[0] = _rope(s, cos, sin, first_half).astype(BF16)
    _store_vt(vb_ref, t[:, LANES:], tile_major=True)
    n_gate = n_cols - off_g
    step = 512
    for j in range(n_gate // step):
        g = proj(off_g + j * step, step)
        gate_ref[0, :, j * step:(j + 1) * step] = jax.nn.sigmoid(g).astype(BF16)


def _context_proj_kernel(x_ref, scale_ref, shift_ref, w_ref, b_ref, gk_ref,
                         ka_ref, va_ref, kb_ref, vb_ref):
    h = (x_ref[0] * (1.0 + scale_ref[0]) + shift_ref[0]).astype(BF16)
    gmat = _head_mean_matrix()
    t = jnp.dot(h, w_ref[...], preferred_element_type=F32) + b_ref[...]
    ka_ref[0] = t[:, 0:LANES].astype(BF16)
    _store_vt(va_ref, t[:, LANES:2 * LANES], tile_major=False)
    kb_ref[0] = _rms_heads(t[:, 2 * LANES:3 * LANES], gk_ref[...], gmat).astype(BF16)
    _store_vt(vb_ref, t[:, 3 * LANES:4 * LANES], tile_major=False)


def _latent_projection(x, scale, shift, w_in, b_in, cos, sin, gq, gk, offs):
    bsz, n_tok, d = x.shape
    n_cols = w_in.shape[1]
    tm = ROW_TILE
    n_q = N_HEADS * HEAD_DIM
    vec = lambda: pl.BlockSpec((1, 1, d), lambda b, i: (b, 0, 0))
    out_shape = (
        jax.ShapeDtypeStruct((bsz, n_q, n_tok), BF16),
        jax.ShapeDtypeStruct((bsz, n_tok, LANES), BF16),
        jax.ShapeDtypeStruct((bsz, N_KV_HEADS, V_ROWS, n_tok), BF16),
        jax.ShapeDtypeStruct((bsz, n_q, n_tok), BF16),
        jax.ShapeDtypeStruct((bsz, n_tok, LANES), BF16),
        jax.ShapeDtypeStruct((bsz, N_KV_HEADS, n_tok // K_TILE, V_ROWS, K_TILE), BF16),
        jax.ShapeDtypeStruct((bsz, n_tok, n_cols - offs[4]), BF16),
    )
    out_specs = (
        pl.BlockSpec((1, n_q, tm), lambda b, i: (b, 0, i)),
        pl.BlockSpec((1, tm, LANES), lambda b, i: (b, i, 0)),
        pl.BlockSpec((1, N_KV_HEADS, V_ROWS, tm), lambda b, i: (b, 0, 0, i)),
        pl.BlockSpec((1, n_q, tm), lambda b, i: (b, 0, i)),
        pl.BlockSpec((1, tm, LANES), lambda b, i: (b, i, 0)),
        pl.BlockSpec((1, N_KV_HEADS, 1, V_ROWS, K_TILE), lambda b, i: (b, 0, i, 0, 0)),
        pl.BlockSpec((1, tm, n_cols - offs[4]), lambda b, i: (b, i, 0)),
    )
    return pl.pallas_call(
        functools.partial(_latent_proj_kernel, offs=offs + (n_cols,)),
        grid=(bsz, n_tok // tm),
        in_specs=[pl.BlockSpec((1, tm, d), lambda b, i: (b, i, 0)), vec(), vec(),
                  _resident((d, n_cols)), _resident((1, n_cols)),
                  pl.BlockSpec((tm, LANES), lambda b, i: (i, 0)),
                  pl.BlockSpec((tm, LANES), lambda b, i: (i, 0)),
                  _resident((1, LANES)), _resident((1, LANES))],
        out_specs=out_specs,
        out_shape=out_shape,
        compiler_params=_params(2),
        name="latent_projection",
    )(x, scale, shift, w_in, b_in, cos, sin, gq, gk)


def _context_projection(ctx, scale, shift, w_kv, b_kv, gk):
    bsz, n_ctx, d = ctx.shape
    n = w_kv.shape[1]
    vec = lambda: pl.BlockSpec((1, 1, d), lambda b: (0, 0, 0))
    kspec = lambda: pl.BlockSpec((1, n_ctx, LANES), lambda b: (b, 0, 0))
    vspec = lambda: pl.BlockSpec((1, N_KV_HEADS, V_ROWS, n_ctx), lambda b: (b, 0, 0, 0))
    kshape = jax.ShapeDtypeStruct((bsz, n_ctx, LANES), BF16)
    vshape = jax.ShapeDtypeStruct((bsz, N_KV_HEADS, V_ROWS, n_ctx), BF16)
    return pl.pallas_call(
        _context_proj_kernel,
        grid=(bsz,),
        in_specs=[pl.BlockSpec((1, n_ctx, d), lambda b: (b, 0, 0)), vec(), vec(),
                  _resident((d, n)), _resident((1, n)), _resident((1, LANES))],
        out_specs=(kspec(), vspec(), kspec(), vspec()),
        out_shape=(kshape, vshape, kshape, vshape),
        compiler_params=_params(1),
        name="context_projection",
    )(ctx, scale, shift, w_kv, b_kv, gk)


def _expand_queries(qx_ref, qt_ref):
    zeros = jnp.zeros((HEAD_DIM, qt_ref.shape[2]), BF16)
    for hh in range(N_HEADS):
        q = qt_ref[0, hh * HEAD_DIM:(hh + 1) * HEAD_DIM, :]
        if hh // GROUPS == 0:
            qx_ref[hh] = jnp.concatenate([q, zeros], axis=0)
        else:
            qx_ref[hh] = jnp.concatenate([zeros, q], axis=0)


def _window_attn_kernel(sink_ref, qt_ref, kp_ref, kc_ref, kn_ref, kx_ref,
                        vp_ref, vc_ref, vn_ref, vx_ref, o_ref, qx_ref, ot_ref, *, n_tok):
    i = pl.program_id(1)
    tq = qt_ref.shape[2]
    _expand_queries(qx_ref, qt_ref)
    k_all = jnp.concatenate([kp_ref[0], kc_ref[0], kn_ref[0], kx_ref[0]], axis=0)
    n_lat = tq + 2 * WINDOW
    n_keys = k_all.shape[0]
    kk = lax.broadcasted_iota(jnp.int32, (n_keys, tq), 0)
    qq = lax.broadcasted_iota(jnp.int32, (n_keys, tq), 1)
    k_pos = i * tq - WINDOW + kk
    rel = kk - WINDOW - qq
    ok = (kk >= n_lat) | ((jnp.abs(rel) <= WINDOW) & (k_pos >= 0) & (k_pos < n_tok))
    bias = jnp.where(ok, 0.0, -jnp.inf).astype(F32)
    for hh in range(N_HEADS):
        kvh = hh // GROUPS
        vt = jnp.concatenate([vp_ref[0, kvh], vc_ref[0, kvh], vn_ref[0, kvh], vx_ref[0, kvh]], axis=1)
        s = jnp.dot(k_all, qx_ref[hh], preferred_element_type=F32) + bias
        sink = sink_ref[hh]
        m = jnp.maximum(jnp.max(s, axis=0, keepdims=True), sink)
        p = jnp.exp(s - m).astype(BF16)
        o = jnp.dot(vt, p, preferred_element_type=F32)
        denom = o[HEAD_DIM:HEAD_DIM + 1] + jnp.exp(sink - m)
        ot_ref[hh * HEAD_DIM:(hh + 1) * HEAD_DIM, :] = o[0:HEAD_DIM] / denom
    o_ref[0] = ot_ref[...].T.astype(BF16)


def _window_attention(sink, qt, k, vt, k_ctx, vt_ctx):
    bsz, n_q, n_tok = qt.shape
    n_ctx = k_ctx.shape[1]
    tq = Q_TILE
    r = tq // WINDOW
    n_wblk = n_tok // WINDOW
    prev = lambda i: jnp.maximum(i * r - 1, 0)
    nxt = lambda i: jnp.minimum((i + 1) * r, n_wblk - 1)
    return pl.pallas_call(
        functools.partial(_window_attn_kernel, n_tok=n_tok),
        grid=(bsz, n_tok // tq),
        in_specs=[pl.BlockSpec(memory_space=pltpu.SMEM),
                  pl.BlockSpec((1, n_q, tq), lambda b, i: (b, 0, i)),
                  pl.BlockSpec((1, WINDOW, LANES), lambda b, i: (b, prev(i), 0)),
                  pl.BlockSpec((1, tq, LANES), lambda b, i: (b, i, 0)),
                  pl.BlockSpec((1, WINDOW, LANES), lambda b, i: (b, nxt(i), 0)),
                  pl.BlockSpec((1, n_ctx, LANES), lambda b, i: (b, 0, 0)),
                  pl.BlockSpec((1, N_KV_HEADS, V_ROWS, WINDOW), lambda b, i: (b, 0, 0, prev(i))),
                  pl.BlockSpec((1, N_KV_HEADS, V_ROWS, tq), lambda b, i: (b, 0, 0, i)),
                  pl.BlockSpec((1, N_KV_HEADS, V_ROWS, WINDOW), lambda b, i: (b, 0, 0, nxt(i))),
                  pl.BlockSpec((1, N_KV_HEADS, V_ROWS, n_ctx), lambda b, i: (b, 0, 0, 0))],
        out_specs=pl.BlockSpec((1, tq, n_q), lambda b, i: (b, i, 0)),
        out_shape=jax.ShapeDtypeStruct((bsz, n_tok, n_q), BF16),
        scratch_shapes=[pltpu.VMEM((N_HEADS, LANES, tq), BF16),
                        pltpu.VMEM((n_q, tq), F32)],
        compiler_params=_params(2),
        name="window_attention",
    )(sink, qt, k, k, k, k_ctx, vt, vt, vt, vt_ctx)


def _global_attn_kernel(qt_ref, k_ref, vt_ref, kx_ref, vx_ref, o_ref, qx_ref, m_ref, acc_ref, ot_ref):
    tq = qt_ref.shape[2]
    n_tiles = vt_ref.shape[2]
    tk = vt_ref.shape[4]
    _expand_queries(qx_ref, qt_ref)
    m_ref[...] = jnp.full(m_ref.shape, -jnp.inf, F32)
    acc_ref[...] = jnp.zeros(acc_ref.shape, F32)

    def key_tile(k, vts):
        ss = [jnp.dot(k, qx_ref[hh], preferred_element_type=F32) for hh in range(N_HEADS)]
        for hh in range(N_HEADS):
            s = ss[hh]
            m_old = m_ref[hh]
            m_new = jnp.maximum(m_old, jnp.max(s, axis=0, keepdims=True))
            p = jnp.exp(s - m_new).astype(BF16)
            pv = jnp.dot(vts[hh // GROUPS], p, preferred_element_type=F32)
            acc_ref[hh] = acc_ref[hh] * jnp.exp(m_old - m_new) + pv
            m_ref[hh] = m_new

    def body(t, carry):
        k = k_ref[0, pl.ds(pl.multiple_of(t * tk, tk), tk), :]
        key_tile(k, [vt_ref[0, kvh, t] for kvh in range(N_KV_HEADS)])
        return carry

    lax.fori_loop(0, n_tiles, body, 0)
    key_tile(kx_ref[0], [vx_ref[0, kvh] for kvh in range(N_KV_HEADS)])
    for hh in range(N_HEADS):
        acc = acc_ref[hh]
        ot_ref[hh * HEAD_DIM:(hh + 1) * HEAD_DIM, :] = acc[0:HEAD_DIM] / acc[HEAD_DIM:HEAD_DIM + 1]
    o_ref[0] = ot_ref[...].T.astype(BF16)


def _global_attention(qt, k, vt, k_ctx, vt_ctx):
    bsz, n_q, n_tok = qt.shape
    n_ctx = k_ctx.shape[1]
    tq = Q_TILE
    n_tiles, tk = vt.shape[2], vt.shape[4]
    return pl.pallas_call(
        _global_attn_kernel,
        grid=(bsz, n_tok // tq),
        in_specs=[pl.BlockSpec((1, n_q, tq), lambda b, i: (b, 0, i)),
                  pl.BlockSpec((1, n_tok, LANES), lambda b, i: (b, 0, 0)),
                  pl.BlockSpec((1, N_KV_HEADS, n_tiles, V_ROWS, tk), lambda b, i: (b, 0, 0, 0, 0)),
                  pl.BlockSpec((1, n_ctx, LANES), lambda b, i: (b, 0, 0)),
                  pl.BlockSpec((1, N_KV_HEADS, V_ROWS, n_ctx), lambda b, i: (b, 0, 0, 0))],
        out_specs=pl.BlockSpec((1, tq, n_q), lambda b, i: (b, i, 0)),
        out_shape=jax.ShapeDtypeStruct((bsz, n_tok, n_q), BF16),
        scratch_shapes=[pltpu.VMEM((N_HEADS, LANES, tq), BF16),
                        pltpu.VMEM((N_HEADS, 1, tq), F32),
                        pltpu.VMEM((N_HEADS, V_ROWS, tq), F32),
                        pltpu.VMEM((n_q, tq), F32)],
        compiler_params=_params(2),
        name="global_attention",
    )(qt, k, vt, k_ctx, vt_ctx)


def _layer_norm(z, g, b):
    mu = jnp.mean(z, axis=-1, keepdims=True)
    zc = z - mu
    var = jnp.mean(zc * zc, axis=-1, keepdims=True)
    return zc * lax.rsqrt(var + LN_EPS) * g + b


def _merge_kernel(oa_ref, ob_ref, gate_ref, x_ref, g1_ref, wa_ref, wb_ref, wo_ref, lng_ref, lnb_ref, o_ref):
    d = x_ref.shape[2]
    a = jnp.dot(oa_ref[0], wa_ref[...], preferred_element_type=F32)
    b = jnp.dot(ob_ref[0], wb_ref[...], preferred_element_type=F32)
    g = gate_ref[0].astype(F32)
    mix = (g[:, :d] * a + g[:, d:] * b).astype(BF16)
    y = jnp.dot(mix, wo_ref[...], preferred_element_type=F32)
    z = DEEPNORM_ALPHA * x_ref[0] + g1_ref[0] * y
    o_ref[0] = _layer_norm(z, lng_ref[...], lnb_ref[...])


def _merge(o_a, o_b, gates, x, gate1, w_a, w_b, w_out, ln_g, ln_b):
    bsz, n_tok, d = x.shape
    n_q = o_a.shape[2]
    tm = ROW_TILE
    row = lambda n: pl.BlockSpec((1, tm, n), lambda b, i: (b, i, 0))
    return pl.pallas_call(
        _merge_kernel,
        grid=(bsz, n_tok // tm),
        in_specs=[row(n_q), row(n_q), row(2 * d), row(d),
                  pl.BlockSpec((1, 1, d), lambda b, i: (b, 0, 0)),
                  _resident((n_q, d)), _resident((n_q, d)), _resident((d, d)),
                  _resident((1, d)), _resident((1, d))],
        out_specs=row(d),
        out_shape=jax.ShapeDtypeStruct((bsz, n_tok, d), F32),
        compiler_params=_params(2),
        name="merge_ln1",
    )(o_a, o_b, gates, x, gate1, w_a, w_b, w_out, ln_g, ln_b)


def _conv_ffn_kernel(x_ref, xp_ref, xn_ref, scale_ref, shift_ref, g2_ref, wup_ref, cw_ref, cb_ref, wdn_ref,
                     lng_ref, lnb_ref, o_ref, acc_ref):
    i = pl.program_id(1)
    n_i = pl.num_programs(1)
    tm = x_ref.shape[1]
    x = x_ref[0]
    scale = 1.0 + scale_ref[0]
    shift = shift_ref[0]
    hp = jnp.where(i > 0, xp_ref[0] * scale + shift, 0.0)
    hn = jnp.where(i < n_i - 1, xn_ref[0] * scale + shift, 0.0)
    h = jnp.concatenate([hp, x * scale + shift, hn], axis=0).astype(BF16)
    acc_ref[...] = jnp.zeros(acc_ref.shape, F32)

    def chunk(c, carry):
        u = jnp.dot(h, wup_ref[c], preferred_element_type=F32)
        w = cw_ref[c]
        n = u.shape[0]
        u_prev = pltpu.roll(u, 1, 0)[HALO:HALO + tm]
        u_next = pltpu.roll(u, n - 1, 0)[HALO:HALO + tm]
        v = w[0:1] * u_prev + w[1:2] * u[HALO:HALO + tm] + w[2:3] * u_next + cb_ref[c]
        gate = v[:, :FF_CHUNK]
        val = v[:, FF_CHUNK:]
        act = (gate * jax.nn.sigmoid(gate) * val).astype(BF16)
        acc_ref[...] += jnp.dot(act, wdn_ref[c], preferred_element_type=F32)
        return carry

    lax.fori_loop(0, wup_ref.shape[0], chunk, 0)
    z = DEEPNORM_ALPHA * x + g2_ref[0] * acc_ref[...]
    o_ref[0] = _layer_norm(z, lng_ref[...], lnb_ref[...])


def _conv_ffn(x, scale, shift, gate2, w_up_c, conv_w_c, conv_b_c, w_down_c, ln_g, ln_b):
    bsz, n_tok, d = x.shape
    tm = ROW_TILE
    n_chunks = w_up_c.shape[0]
    hb = tm // HALO
    n_hblk = n_tok // HALO
    vec = lambda: pl.BlockSpec((1, 1, d), lambda b, i: (b, 0, 0))
    return pl.pallas_call(
        _conv_ffn_kernel,
        grid=(bsz, n_tok // tm),
        in_specs=[pl.BlockSpec((1, tm, d), lambda b, i: (b, i, 0)),
                  pl.BlockSpec((1, HALO, d), lambda b, i: (b, jnp.maximum(i * hb - 1, 0), 0)),
                  pl.BlockSpec((1, HALO, d), lambda b, i: (b, jnp.minimum((i + 1) * hb, n_hblk - 1), 0)),
                  vec(), vec(), vec(),
                  _resident((n_chunks, d, 2 * FF_CHUNK)),
                  _resident((n_chunks, CONV_WIDTH, 2 * FF_CHUNK)),
                  _resident((n_chunks, 1, 2 * FF_CHUNK)),
                  _resident((n_chunks, FF_CHUNK, d)),
                  _resident((1, d)), _resident((1, d))],
        out_specs=pl.BlockSpec((1, tm, d), lambda b, i: (b, i, 0)),
        out_shape=jax.ShapeDtypeStruct((bsz, n_tok, d), F32),
        scratch_shapes=[pltpu.VMEM((tm, d), F32)],
        compiler_params=_params(2),
        name="conv_ffn_ln2",
    )(x, x, x, scale, shift, gate2, w_up_c, conv_w_c, conv_b_c, w_down_c, ln_g, ln_b)


def _rope_tables(n_tok):
    pos = jnp.arange(n_tok, dtype=jnp.int32)
    rows = (pos // GRID_W).astype(F32)
    cols = (pos % GRID_W).astype(F32)
    n_freq = HEAD_DIM // 4
    inv_freq = ROPE_THETA ** (-jnp.arange(n_freq, dtype=F32) / n_freq)
    ang_r = rows[:, None] * inv_freq
    ang_c = cols[:, None] * inv_freq
    cos = jnp.concatenate([jnp.cos(ang_r)] * 2 + [jnp.cos(ang_c)] * 2, axis=-1)
    sin = jnp.concatenate([-jnp.sin(ang_r), jnp.sin(ang_r), -jnp.sin(ang_c), jnp.sin(ang_c)], axis=-1)
    reps = LANES // HEAD_DIM
    return jnp.tile(cos, (1, reps)), jnp.tile(sin, (1, reps))


def _chunk_cols(a, n_chunks):
    lead = a.shape[:-1]
    return jnp.moveaxis(a.reshape(lead + (n_chunks, FF_CHUNK)), -2, 0)


def kernel(x, c, ctx, c_ctx, w_mod, b_mod, w_in, b_in, attn_sink, q_norm_g, k_norm_g, w_branch_a, w_branch_b,
           w_out, ln1_g, ln1_b, w_up, conv_w, conv_b, w_down, ln2_g, ln2_b):
    bsz, n_tok, d = x.shape
    n_q = N_HEADS * HEAD_DIM
    n_kv = N_KV_HEADS * HEAD_DIM
    off_qa = 0
    off_ka = off_qa + n_q
    off_qb = off_ka + 2 * n_kv
    off_kb = off_qb + n_q
    off_g = off_kb + 2 * n_kv
    d_ff = w_down.shape[1]
    n_chunks = d_ff // FF_CHUNK
    assert n_tok % ROW_TILE == 0 and n_tok % K_TILE == 0 and n_tok % GRID_W == 0
    assert d_ff % FF_CHUNK == 0 and bsz <= SUBLANES - 1
    assert w_mod.shape[0] == DEPTH == 1 and ROW_TILE == K_TILE

    cos, sin = _rope_tables(n_tok)
    tile2 = lambda g: jnp.tile(g, LANES // HEAD_DIM)[None, :]
    c_rows = jnp.zeros((SUBLANES, d), F32).at[:bsz].set(c).at[bsz].set(c_ctx)
    mod = _modulation(c_rows, w_mod[0].astype(BF16), b_mod[0][None, :])
    shift1, scale1, gate1, shift2, scale2, gate2 = [mod[:bsz, None, j * d:(j + 1) * d] for j in range(N_MOD)]
    shift_c = mod[bsz:bsz + 1, None, 0:d]
    scale_c = mod[bsz:bsz + 1, None, d:2 * d]

    w_in_l = w_in[0].astype(BF16)
    b_in_l = b_in[0][None, :]
    gq, gk = tile2(q_norm_g[0]), tile2(k_norm_g[0])
    kv_cols = lambda a: jnp.concatenate([a[:, off_ka:off_qb], a[:, off_kb:off_g]], axis=1)
    kc_a, vc_a, kc_b, vc_b = _context_projection(ctx, scale_c, shift_c, kv_cols(w_in_l), kv_cols(b_in_l), gk)
    qa_t, k_a, va_t, qb_t, k_b, vb_t, gates = _latent_projection(
        x, scale1, shift1, w_in_l, b_in_l, cos, sin, gq, gk, (off_qa, off_ka, off_qb, off_kb, off_g))

    o_a = _window_attention(attn_sink[0], qa_t, k_a, va_t, kc_a, vc_a)
    o_b = _global_attention(qb_t, k_b, vb_t, kc_b, vc_b)

    x_mid = _merge(o_a, o_b, gates, x, gate1, w_branch_a[0].astype(BF16), w_branch_b[0].astype(BF16),
                   w_out[0].astype(BF16), ln1_g[0][None, :], ln1_b[0][None, :])

    pair = lambda a: jnp.concatenate([_chunk_cols(a[..., :d_ff], n_chunks), _chunk_cols(a[..., d_ff:], n_chunks)],
                                     axis=-1)
    w_down_c = w_down[0].astype(BF16).reshape(n_chunks, FF_CHUNK, d)
    return _conv_ffn(x_mid, scale2, shift2, gate2, pair(w_up[0].astype(BF16)), pair(conv_w[0]),
                     pair(conv_b[0][None, :]), w_down_c, ln2_g[0][None, :], ln2_b[0][None, :])
```

```python
import functools

import jax
import jax.numpy as jnp
from jax import lax
from jax.experimental import pallas as pl
from jax.experimental.pallas import tpu as pltpu

F32 = jnp.float32
BF16 = jnp.bfloat16

GRID_W = 64
HEAD_DIM = 64
N_HEADS = 8
N_KV_HEADS = 2
GROUPS = N_HEADS // N_KV_HEADS
WINDOW = 128
ROPE_THETA = 10000.0
LN_EPS = 1e-5
QK_EPS = 1e-6
N_MOD = 6
DEPTH = 1
DEEPNORM_ALPHA = (2.0 * DEPTH) ** 0.25
CONV_WIDTH = 3
LOG2_E = 1.4426950408889634
Q_SCALE = HEAD_DIM ** -0.5 * LOG2_E

LANES = 128
SUBLANES = 8
VMEM_LIMIT_BYTES = 56 * 1024 * 1024

ROW_TILE = 512
FFN_ROW_TILE = 512
Q_TILE = 256
K_TILE = 512
V_ROWS = HEAD_DIM + 16
FF_CHUNK = 256
HALO = SUBLANES


def _params(n_grid):
    return pltpu.CompilerParams(dimension_semantics=("arbitrary",) * n_grid,
                                vmem_limit_bytes=VMEM_LIMIT_BYTES)


def _resident(shape):
    zeros = (0,) * len(shape)
    return pl.BlockSpec(shape, lambda *_: zeros, pipeline_mode=pl.Buffered(1))


def _mod_kernel(c_ref, w_ref, b_ref, o_ref):
    c = c_ref[...]
    a = (c * jax.nn.sigmoid(c)).astype(BF16)
    o_ref[...] = jnp.dot(a, w_ref[...], preferred_element_type=F32) + b_ref[...]


def _modulation(c_rows, w_mod, b_mod):
    rows, d = c_rows.shape
    n = w_mod.shape[1]
    tn = 1024
    return pl.pallas_call(
        _mod_kernel,
        grid=(n // tn,),
        in_specs=[pl.BlockSpec((rows, d), lambda j: (0, 0)),
                  pl.BlockSpec((d, tn), lambda j: (0, j)),
                  pl.BlockSpec((1, tn), lambda j: (0, j))],
        out_specs=pl.BlockSpec((rows, tn), lambda j: (0, j)),
        out_shape=jax.ShapeDtypeStruct((rows, n), F32),
        compiler_params=_params(1),
        name="modulation",
    )(c_rows, w_mod, b_mod)


def _head_mean_matrix():
    r = lax.broadcasted_iota(jnp.int32, (LANES, LANES), 0) // HEAD_DIM
    c = lax.broadcasted_iota(jnp.int32, (LANES, LANES), 1) // HEAD_DIM
    return jnp.where(r == c, 1.0 / HEAD_DIM, 0.0).astype(BF16)


def _rms_heads(t, gain, gmat):
    ms = jnp.dot((t * t).astype(BF16), gmat, preferred_element_type=F32)
    return t * lax.rsqrt(ms + QK_EPS) * gain


def _rope(t, cos, sin_signed, first_half):
    partner = jnp.where(first_half, pltpu.roll(t, LANES - 16, 1), pltpu.roll(t, 16, 1))
    return t * cos + partner * sin_signed


def _store_vt(vt_ref, v, tile_major):
    vt = v.T.astype(BF16)
    ones = jnp.ones((V_ROWS - HEAD_DIM, v.shape[0]), BF16)
    for kvh in range(N_KV_HEADS):
        blk = vt[kvh * HEAD_DIM:(kvh + 1) * HEAD_DIM]
        if tile_major:
            vt_ref[0, kvh, 0, 0:HEAD_DIM, :] = blk
            vt_ref[0, kvh, 0, HEAD_DIM:V_ROWS, :] = ones
        else:
            vt_ref[0, kvh, 0:HEAD_DIM, :] = blk
            vt_ref[0, kvh, HEAD_DIM:V_ROWS, :] = ones


def _latent_proj_kernel(x_ref, scale_ref, shift_ref, w_ref, b_ref, cos_ref, sin_ref, gq_ref, gk_ref,
                        qa_ref, ka_ref, va_ref, qb_ref, kb_ref, vb_ref, gate_ref, *, offs):
    off_qa, off_ka, off_qb, off_kb, off_g, n_cols = offs
    h = (x_ref[0] * (1.0 + scale_ref[0]) + shift_ref[0]).astype(BF16)
    cos = cos_ref[...]
    sin = sin_ref[...]
    lane = lax.broadcasted_iota(jnp.int32, cos.shape, 1)
    first_half = (lane & 31) < 16
    gmat = _head_mean_matrix()
    q_scale = Q_SCALE

    def proj(lo, n):
        return jnp.dot(h, w_ref[:, lo:lo + n], preferred_element_type=F32) + b_ref[:, lo:lo + n]

    n_q = N_HEADS * HEAD_DIM
    t = proj(off_qa, n_q)
    for j in range(n_q // LANES):
        s = _rope(t[:, j * LANES:(j + 1) * LANES], cos, sin, first_half) * q_scale
        qa_ref[0, j * LANES:(j + 1) * LANES, :] = s.T.astype(BF16)
    t = proj(off_ka, 2 * LANES)
    ka_ref[0] = _rope(t[:, :LANES], cos, sin, first_half).astype(BF16)
    _store_vt(va_ref, t[:, LANES:], tile_major=False)
    t = proj(off_qb, n_q)
    for j in range(n_q // LANES):
        s = _rms_heads(t[:, j * LANES:(j + 1) * LANES], gq_ref[...], gmat)
        s = _rope(s, cos, sin, first_half) * q_scale
        qb_ref[0, j * LANES:(j + 1) * LANES, :] = s.T.astype(BF16)
    t = proj(off_kb, 2 * LANES)
    s = _rms_heads(t[:, :LANES], gk_ref[...], gmat)
    kb_ref[0] = _rope(s, cos, sin, first_half).astype(BF16)
    _store_vt(vb_ref, t[:, LANES:], tile_major=True)
    n_gate = n_cols - off_g
    step = 512
    for j in range(n_gate // step):
        g = proj(off_g + j * step, step)
        gate_ref[0, :, j * step:(j + 1) * step] = jax.nn.sigmoid(g).astype(BF16)


def _context_proj_kernel(x_ref, scale_ref, shift_ref, w_ref, b_ref, gk_ref,
                         ka_ref, va_ref, kb_ref, vb_ref):
    h = (x_ref[0] * (1.0 + scale_ref[0]) + shift_ref[0]).astype(BF16)
    gmat = _head_mean_matrix()
    t = jnp.dot(h, w_ref[...], preferred_element_type=F32) + b_ref[...]
    ka_ref[0] = t[:, 0:LANES].astype(BF16)
    _store_vt(va_ref, t[:, LANES:2 * LANES], tile_major=False)
    kb_ref[0] = _rms_heads(t[:, 2 * LANES:3 * LANES], gk_ref[...], gmat).astype(BF16)
    _store_vt(vb_ref, t[:, 3 * LANES:4 * LANES], tile_major=False)


def _latent_projection(x, scale, shift, w_in, b_in, cos, sin, gq, gk, offs):
    bsz, n_tok, d = x.shape
    n_cols = w_in.shape[1]
    tm = ROW_TILE
    n_q = N_HEADS * HEAD_DIM
    vec = lambda: pl.BlockSpec((1, 1, d), lambda b, i: (b, 0, 0))
    out_shape = (
        jax.ShapeDtypeStruct((bsz, n_q, n_tok), BF16),
        jax.ShapeDtypeStruct((bsz, n_tok, LANES), BF16),
        jax.ShapeDtypeStruct((bsz, N_KV_HEADS, V_ROWS, n_tok), BF16),
        jax.ShapeDtypeStruct((bsz, n_q, n_tok), BF16),
        jax.ShapeDtypeStruct((bsz, n_tok, LANES), BF16),
        jax.ShapeDtypeStruct((bsz, N_KV_HEADS, n_tok // K_TILE, V_ROWS, K_TILE), BF16),
        jax.ShapeDtypeStruct((bsz, n_tok, n_cols - offs[4]), BF16),
    )
    out_specs = (
        pl.BlockSpec((1, n_q, tm), lambda b, i: (b, 0, i)),
        pl.BlockSpec((1, tm, LANES), lambda b, i: (b, i, 0)),
        pl.BlockSpec((1, N_KV_HEADS, V_ROWS, tm), lambda b, i: (b, 0, 0, i)),
        pl.BlockSpec((1, n_q, tm), lambda b, i: (b, 0, i)),
        pl.BlockSpec((1, tm, LANES), lambda b, i: (b, i, 0)),
        pl.BlockSpec((1, N_KV_HEADS, 1, V_ROWS, K_TILE), lambda b, i: (b, 0, i, 0, 0)),
        pl.BlockSpec((1, tm, n_cols - offs[4]), lambda b, i: (b, i, 0)),
    )
    return pl.pallas_call(
        functools.partial(_latent_proj_kernel, offs=offs + (n_cols,)),
        grid=(bsz, n_tok // tm),
        in_specs=[pl.BlockSpec((1, tm, d), lambda b, i: (b, i, 0)), vec(), vec(),
                  _resident((d, n_cols)), _resident((1, n_cols)),
                  pl.BlockSpec((tm, LANES), lambda b, i: (i, 0)),
                  pl.BlockSpec((tm, LANES), lambda b, i: (i, 0)),
                  _resident((1, LANES)), _resident((1, LANES))],
        out_specs=out_specs,
        out_shape=out_shape,
        compiler_params=_params(2),
        name="latent_projection",
    )(x, scale, shift, w_in, b_in, cos, sin, gq, gk)


def _context_projection(ctx, scale, shift, w_kv, b_kv, gk):
    bsz, n_ctx, d = ctx.shape
    n = w_kv.shape[1]
    vec = lambda: pl.BlockSpec((1, 1, d), lambda b: (0, 0, 0))
    kspec = lambda: pl.BlockSpec((1, n_ctx, LANES), lambda b: (b, 0, 0))
    vspec = lambda: pl.BlockSpec((1, N_KV_HEADS, V_ROWS, n_ctx), lambda b: (b, 0, 0, 0))
    kshape = jax.ShapeDtypeStruct((bsz, n_ctx, LANES), BF16)
    vshape = jax.ShapeDtypeStruct((bsz, N_KV_HEADS, V_ROWS, n_ctx), BF16)
    return pl.pallas_call(
        _context_proj_kernel,
        grid=(bsz,),
        in_specs=[pl.BlockSpec((1, n_ctx, d), lambda b: (b, 0, 0)), vec(), vec(),
                  _resident((d, n)), _resident((1, n)), _resident((1, LANES))],
        out_specs=(kspec(), vspec(), kspec(), vspec()),
        out_shape=(kshape, vshape, kshape, vshape),
        compiler_params=_params(1),
        name="context_projection",
    )(ctx, scale, shift, w_kv, b_kv, gk)


def _expand_queries(qx_ref, qt_ref):
    zeros = jnp.zeros((HEAD_DIM, qt_ref.shape[2]), BF16)
    for hh in range(N_HEADS):
        q = qt_ref[0, hh * HEAD_DIM:(hh + 1) * HEAD_DIM, :]
        if hh // GROUPS == 0:
            qx_ref[hh] = jnp.concatenate([q, zeros], axis=0)
        else:
            qx_ref[hh] = jnp.concatenate([zeros, q], axis=0)


def _window_attn_kernel(sink_ref, qt_ref, kp_ref, kc_ref, kn_ref, kx_ref,
                        vp_ref, vc_ref, vn_ref, vx_ref, o_ref, qx_ref, ot_ref, *, n_tok):
    i = pl.program_id(1)
    tq = qt_ref.shape[2]
    _expand_queries(qx_ref, qt_ref)
    k_all = jnp.concatenate([kp_ref[0], kc_ref[0], kn_ref[0], kx_ref[0]], axis=0)
    n_lat = tq + 2 * WINDOW
    n_keys = k_all.shape[0]
    kk = lax.broadcasted_iota(jnp.int32, (n_keys, tq), 0)
    qq = lax.broadcasted_iota(jnp.int32, (n_keys, tq), 1)
    k_pos = i * tq - WINDOW + kk
    rel = kk - WINDOW - qq
    ok = (kk >= n_lat) | ((jnp.abs(rel) <= WINDOW) & (k_pos >= 0) & (k_pos < n_tok))
    vts = [jnp.concatenate([vp_ref[0, kvh], vc_ref[0, kvh], vn_ref[0, kvh], vx_ref[0, kvh]], axis=1)
           for kvh in range(N_KV_HEADS)]
    ss = [jnp.dot(k_all, qx_ref[hh], preferred_element_type=F32) for hh in range(N_HEADS)]
    for hh in range(N_HEADS):
        s = jnp.where(ok, ss[hh], -jnp.inf)
        sink = sink_ref[hh] * LOG2_E
        m = jnp.maximum(jnp.max(s, axis=0, keepdims=True), sink)
        p = jnp.exp2(s - m).astype(BF16)
        o = jnp.dot(vts[hh // GROUPS], p, preferred_element_type=F32)
        denom = o[HEAD_DIM:HEAD_DIM + 1] + jnp.exp2(sink - m)
        ot_ref[hh * HEAD_DIM:(hh + 1) * HEAD_DIM, :] = o[0:HEAD_DIM] / denom
    o_ref[0] = ot_ref[...].T.astype(BF16)


def _window_attention(sink, qt, k, vt, k_ctx, vt_ctx):
    bsz, n_q, n_tok = qt.shape
    n_ctx = k_ctx.shape[1]
    tq = Q_TILE
    r = tq // WINDOW
    n_wblk = n_tok // WINDOW
    prev = lambda i: jnp.maximum(i * r - 1, 0)
    nxt = lambda i: jnp.minimum((i + 1) * r, n_wblk - 1)
    return pl.pallas_call(
        functools.partial(_window_attn_kernel, n_tok=n_tok),
        grid=(bsz, n_tok // tq),
        in_specs=[pl.BlockSpec(memory_space=pltpu.SMEM),
                  pl.BlockSpec((1, n_q, tq), lambda b, i: (b, 0, i)),
                  pl.BlockSpec((1, WINDOW, LANES), lambda b, i: (b, prev(i), 0)),
                  pl.BlockSpec((1, tq, LANES), lambda b, i: (b, i, 0)),
                  pl.BlockSpec((1, WINDOW, LANES), lambda b, i: (b, nxt(i), 0)),
                  pl.BlockSpec((1, n_ctx, LANES), lambda b, i: (b, 0, 0)),
                  pl.BlockSpec((1, N_KV_HEADS, V_ROWS, WINDOW), lambda b, i: (b, 0, 0, prev(i))),
                  pl.BlockSpec((1, N_KV_HEADS, V_ROWS, tq), lambda b, i: (b, 0, 0, i)),
                  pl.BlockSpec((1, N_KV_HEADS, V_ROWS, WINDOW), lambda b, i: (b, 0, 0, nxt(i))),
                  pl.BlockSpec((1, N_KV_HEADS, V_ROWS, n_ctx), lambda b, i: (b, 0, 0, 0))],
        out_specs=pl.BlockSpec((1, tq, n_q), lambda b, i: (b, i, 0)),
        out_shape=jax.ShapeDtypeStruct((bsz, n_tok, n_q), BF16),
        scratch_shapes=[pltpu.VMEM((N_HEADS, LANES, tq), BF16),
                        pltpu.VMEM((n_q, tq), F32)],
        compiler_params=_params(2),
        name="window_attention",
    )(sink, qt, k, k, k, k_ctx, vt, vt, vt, vt_ctx)


def _global_attn_kernel(qt_ref, k_ref, vt_ref, kx_ref, vx_ref, o_ref,
                        qx_ref, m_ref, acc_ref, ot_ref, s0_ref, s1_ref, mt0_ref, mt1_ref):
    n_tiles = vt_ref.shape[2]
    tk = vt_ref.shape[4]
    _expand_queries(qx_ref, qt_ref)
    m_ref[...] = jnp.full(m_ref.shape, -jnp.inf, F32)
    acc_ref[...] = jnp.zeros(acc_ref.shape, F32)

    def scores(hh, k, s_ref, mt_ref):
        s = jnp.dot(k, qx_ref[hh], preferred_element_type=F32)
        s_ref[hh, 0:k.shape[0], :] = s
        mt_ref[hh] = jnp.max(s, axis=0, keepdims=True)

    def accumulate(hh, vt, n, s_ref, mt_ref):
        m_old = m_ref[hh]
        m_new = jnp.maximum(m_old, mt_ref[hh])
        p = jnp.exp2(s_ref[hh, 0:n, :] - m_new).astype(BF16)
        pv = jnp.dot(vt, p, preferred_element_type=F32)
        acc_ref[hh] = acc_ref[hh] * jnp.exp2(m_old - m_new) + pv
        m_ref[hh] = m_new

    def latent_keys(t):
        return k_ref[0, pl.ds(pl.multiple_of(t * tk, tk), tk), :]

    for hh in range(N_HEADS):
        scores(hh, kx_ref[0], s1_ref, mt1_ref)
    for hh in range(N_HEADS):
        scores(hh, latent_keys(0), s0_ref, mt0_ref)
        accumulate(hh, vx_ref[0, hh // GROUPS], kx_ref.shape[1], s1_ref, mt1_ref)

    def body(j, carry):
        t = 2 * j
        k_odd = latent_keys(t + 1)
        k_even = latent_keys(jnp.minimum(t + 2, n_tiles - 1))
        for hh in range(N_HEADS):
            scores(hh, k_odd, s1_ref, mt1_ref)
            accumulate(hh, vt_ref[0, hh // GROUPS, t], tk, s0_ref, mt0_ref)
        for hh in range(N_HEADS):
            scores(hh, k_even, s0_ref, mt0_ref)
            accumulate(hh, vt_ref[0, hh // GROUPS, t + 1], tk, s1_ref, mt1_ref)
        return carry

    lax.fori_loop(0, n_tiles // 2, body, 0)
    for hh in range(N_HEADS):
        acc = acc_ref[hh]
        ot_ref[hh * HEAD_DIM:(hh + 1) * HEAD_DIM, :] = acc[0:HEAD_DIM] / acc[HEAD_DIM:HEAD_DIM + 1]
    o_ref[0] = ot_ref[...].T.astype(BF16)


def _global_attention(qt, k, vt, k_ctx, vt_ctx):
    bsz, n_q, n_tok = qt.shape
    n_ctx = k_ctx.shape[1]
    tq = Q_TILE
    n_tiles, tk = vt.shape[2], vt.shape[4]
    assert n_tiles % 2 == 0 and n_ctx <= tk
    stage = lambda: [pltpu.VMEM((N_HEADS, tk, tq), F32), pltpu.VMEM((N_HEADS, 1, tq), F32)]
    s0, mt0 = stage()
    s1, mt1 = stage()
    return pl.pallas_call(
        _global_attn_kernel,
        grid=(bsz, n_tok // tq),
        in_specs=[pl.BlockSpec((1, n_q, tq), lambda b, i: (b, 0, i)),
                  pl.BlockSpec((1, n_tok, LANES), lambda b, i: (b, 0, 0)),
                  pl.BlockSpec((1, N_KV_HEADS, n_tiles, V_ROWS, tk), lambda b, i: (b, 0, 0, 0, 0)),
                  pl.BlockSpec((1, n_ctx, LANES), lambda b, i: (b, 0, 0)),
                  pl.BlockSpec((1, N_KV_HEADS, V_ROWS, n_ctx), lambda b, i: (b, 0, 0, 0))],
        out_specs=pl.BlockSpec((1, tq, n_q), lambda b, i: (b, i, 0)),
        out_shape=jax.ShapeDtypeStruct((bsz, n_tok, n_q), BF16),
        scratch_shapes=[pltpu.VMEM((N_HEADS, LANES, tq), BF16),
                        pltpu.VMEM((N_HEADS, 1, tq), F32),
                        pltpu.VMEM((N_HEADS, V_ROWS, tq), F32),
                        pltpu.VMEM((n_q, tq), F32),
                        s0, s1, mt0, mt1],
        compiler_params=_params(2),
        name="global_attention",
    )(qt, k, vt, k_ctx, vt_ctx)


def _layer_norm(z, g, b):
    mu = jnp.mean(z, axis=-1, keepdims=True)
    zc = z - mu
    var = jnp.mean(zc * zc, axis=-1, keepdims=True)
    return zc * lax.rsqrt(var + LN_EPS) * g + b


def _merge_kernel(oa_ref, ob_ref, gate_ref, x_ref, g1_ref, wa_ref, wb_ref, wo_ref, lng_ref, lnb_ref, o_ref):
    d = x_ref.shape[2]
    a = jnp.dot(oa_ref[0], wa_ref[...], preferred_element_type=F32)
    b = jnp.dot(ob_ref[0], wb_ref[...], preferred_element_type=F32)
    g = gate_ref[0].astype(F32)
    mix = (g[:, :d] * a + g[:, d:] * b).astype(BF16)
    y = jnp.dot(mix, wo_ref[...], preferred_element_type=F32)
    z = DEEPNORM_ALPHA * x_ref[0] + g1_ref[0] * y
    o_ref[0] = _layer_norm(z, lng_ref[...], lnb_ref[...])


def _merge(o_a, o_b, gates, x, gate1, w_a, w_b, w_out, ln_g, ln_b):
    bsz, n_tok, d = x.shape
    n_q = o_a.shape[2]
    tm = ROW_TILE
    row = lambda n: pl.BlockSpec((1, tm, n), lambda b, i: (b, i, 0))
    return pl.pallas_call(
        _merge_kernel,
        grid=(bsz, n_tok // tm),
        in_specs=[row(n_q), row(n_q), row(2 * d), row(d),
                  pl.BlockSpec((1, 1, d), lambda b, i: (b, 0, 0)),
                  _resident((n_q, d)), _resident((n_q, d)), _resident((d, d)),
                  _resident((1, d)), _resident((1, d))],
        out_specs=row(d),
        out_shape=jax.ShapeDtypeStruct((bsz, n_tok, d), F32),
        compiler_params=_params(2),
        name="merge_ln1",
    )(o_a, o_b, gates, x, gate1, w_a, w_b, w_out, ln_g, ln_b)


def _conv_ffn_kernel(x_ref, xp_ref, xn_ref, scale_ref, shift_ref, g2_ref, wup_ref, cw_ref, cb_ref, wdn_ref,
                     lng_ref, lnb_ref, o_ref, acc_ref, h_ref, u0_ref, u1_ref):
    i = pl.program_id(1)
    n_i = pl.num_programs(1)
    tm = x_ref.shape[1]
    n_chunks = wup_ref.shape[0]
    scale = 1.0 + scale_ref[0]
    shift = shift_ref[0]
    hp = jnp.where(i > 0, xp_ref[0] * scale + shift, 0.0)
    hn = jnp.where(i < n_i - 1, xn_ref[0] * scale + shift, 0.0)
    h_ref[...] = jnp.concatenate([hp, x_ref[0] * scale + shift, hn], axis=0).astype(BF16)
    acc_ref[...] = jnp.zeros(acc_ref.shape, F32)

    n_ext = tm + 2 * HALO
    half = tm // 2
    up_rows = ((0, half + 2 * HALO), (half + 2 * HALO, n_ext - half - 2 * HALO))
    down_rows = ((0, half), (half, tm - half))

    def up(c, u_ref, lo, n):
        u_ref[lo:lo + n, :] = jnp.dot(h_ref[lo:lo + n, :], wup_ref[c], preferred_element_type=F32)

    def down(c, u_ref, lo, n):
        w = cw_ref[c]
        r = HALO + lo
        v = (w[0:1] * u_ref[r - 1:r - 1 + n, :] + w[1:2] * u_ref[r:r + n, :] + w[2:3] * u_ref[r + 1:r + 1 + n, :]
             + cb_ref[c])
        gate = v[:, :FF_CHUNK]
        val = v[:, FF_CHUNK:]
        act = (gate * jax.nn.sigmoid(gate) * val).astype(BF16)
        acc_ref[lo:lo + n, :] += jnp.dot(act, wdn_ref[c], preferred_element_type=F32)

    def stage(c_up, u_up, c_down, u_down):
        for (ulo, un), (dlo, dn) in zip(up_rows, down_rows):
            if c_up is not None:
                up(c_up, u_up, ulo, un)
            if c_down is not None:
                down(c_down, u_down, dlo, dn)

    stage(0, u0_ref, None, None)

    def pair(j, carry):
        c = 2 * j
        stage(c + 1, u1_ref, c, u0_ref)
        stage(c + 2, u0_ref, c + 1, u1_ref)
        return carry

    lax.fori_loop(0, (n_chunks - 1) // 2, pair, 0)
    if n_chunks % 2 == 0:
        stage(n_chunks - 1, u1_ref, n_chunks - 2, u0_ref)
        stage(None, None, n_chunks - 1, u1_ref)
    else:
        stage(None, None, n_chunks - 1, u0_ref)
    z = DEEPNORM_ALPHA * x_ref[0] + g2_ref[0] * acc_ref[...]
    o_ref[0] = _layer_norm(z, lng_ref[...], lnb_ref[...])


def _conv_ffn(x, scale, shift, gate2, w_up_c, conv_w_c, conv_b_c, w_down_c, ln_g, ln_b):
    bsz, n_tok, d = x.shape
    tm = FFN_ROW_TILE
    n_chunks = w_up_c.shape[0]
    hb = tm // HALO
    n_hblk = n_tok // HALO
    vec = lambda: pl.BlockSpec((1, 1, d), lambda b, i: (b, 0, 0))
    return pl.pallas_call(
        _conv_ffn_kernel,
        grid=(bsz, n_tok // tm),
        in_specs=[pl.BlockSpec((1, tm, d), lambda b, i: (b, i, 0)),
                  pl.BlockSpec((1, HALO, d), lambda b, i: (b, jnp.maximum(i * hb - 1, 0), 0)),
                  pl.BlockSpec((1, HALO, d), lambda b, i: (b, jnp.minimum((i + 1) * hb, n_hblk - 1), 0)),
                  vec(), vec(), vec(),
                  _resident((n_chunks, d, 2 * FF_CHUNK)),
                  _resident((n_chunks, CONV_WIDTH, 2 * FF_CHUNK)),
                  _resident((n_chunks, 1, 2 * FF_CHUNK)),
                  _resident((n_chunks, FF_CHUNK, d)),
                  _resident((1, d)), _resident((1, d))],
        out_specs=pl.BlockSpec((1, tm, d), lambda b, i: (b, i, 0)),
        out_shape=jax.ShapeDtypeStruct((bsz, n_tok, d), F32),
        scratch_shapes=[pltpu.VMEM((tm, d), F32),
                        pltpu.VMEM((tm + 2 * HALO, d), BF16),
                        pltpu.VMEM((tm + 2 * HALO, 2 * FF_CHUNK), F32),
                        pltpu.VMEM((tm + 2 * HALO, 2 * FF_CHUNK), F32)],
        compiler_params=_params(2),
        name="conv_ffn_ln2",
    )(x, x, x, scale, shift, gate2, w_up_c, conv_w_c, conv_b_c, w_down_c, ln_g, ln_b)


def _rope_tables(n_tok):
    pos = jnp.arange(n_tok, dtype=jnp.int32)
    rows = (pos // GRID_W).astype(F32)
    cols = (pos % GRID_W).astype(F32)
    n_freq = HEAD_DIM // 4
    inv_freq = ROPE_THETA ** (-jnp.arange(n_freq, dtype=F32) / n_freq)
    ang_r = rows[:, None] * inv_freq
    ang_c = cols[:, None] * inv_freq
    cos = jnp.concatenate([jnp.cos(ang_r)] * 2 + [jnp.cos(ang_c)] * 2, axis=-1)
    sin = jnp.concatenate([-jnp.sin(ang_r), jnp.sin(ang_r), -jnp.sin(ang_c), jnp.sin(ang_c)], axis=-1)
    reps = LANES // HEAD_DIM
    return jnp.tile(cos, (1, reps)), jnp.tile(sin, (1, reps))


def _chunk_cols(a, n_chunks):
    lead = a.shape[:-1]
    return jnp.moveaxis(a.reshape(lead + (n_chunks, FF_CHUNK)), -2, 0)


def kernel(x, c, ctx, c_ctx, w_mod, b_mod, w_in, b_in, attn_sink, q_norm_g, k_norm_g, w_branch_a, w_branch_b,
           w_out, ln1_g, ln1_b, w_up, conv_w, conv_b, w_down, ln2_g, ln2_b):
    bsz, n_tok, d = x.shape
    n_q = N_HEADS * HEAD_DIM
    n_kv = N_KV_HEADS * HEAD_DIM
    off_qa = 0
    off_ka = off_qa + n_q
    off_qb = off_ka + 2 * n_kv
    off_kb = off_qb + n_q
    off_g = off_kb + 2 * n_kv
    d_ff = w_down.shape[1]
    n_chunks = d_ff // FF_CHUNK
    assert n_tok % ROW_TILE == 0 and n_tok % K_TILE == 0 and n_tok % GRID_W == 0
    assert d_ff % FF_CHUNK == 0 and bsz <= SUBLANES - 1
    assert w_mod.shape[0] == DEPTH == 1 and ROW_TILE == K_TILE

    cos, sin = _rope_tables(n_tok)
    tile2 = lambda g: jnp.tile(g, LANES // HEAD_DIM)[None, :]
    c_rows = jnp.zeros((SUBLANES, d), F32).at[:bsz].set(c).at[bsz].set(c_ctx)
    mod = _modulation(c_rows, w_mod[0].astype(BF16), b_mod[0][None, :])
    shift1, scale1, gate1, shift2, scale2, gate2 = [mod[:bsz, None, j * d:(j + 1) * d] for j in range(N_MOD)]
    shift_c = mod[bsz:bsz + 1, None, 0:d]
    scale_c = mod[bsz:bsz + 1, None, d:2 * d]

    w_in_l = w_in[0].astype(BF16)
    b_in_l = b_in[0][None, :]
    gq, gk = tile2(q_norm_g[0]), tile2(k_norm_g[0])
    kv_cols = lambda a: jnp.concatenate([a[:, off_ka:off_qb], a[:, off_kb:off_g]], axis=1)
    kc_a, vc_a, kc_b, vc_b = _context_projection(ctx, scale_c, shift_c, kv_cols(w_in_l), kv_cols(b_in_l), gk)
    qa_t, k_a, va_t, qb_t, k_b, vb_t, gates = _latent_projection(
        x, scale1, shift1, w_in_l, b_in_l, cos, sin, gq, gk, (off_qa, off_ka, off_qb, off_kb, off_g))

    o_a = _window_attention(attn_sink[0], qa_t, k_a, va_t, kc_a, vc_a)
    o_b = _global_attention(qb_t, k_b, vb_t, kc_b, vc_b)

    x_mid = _merge(o_a, o_b, gates, x, gate1, w_branch_a[0].astype(BF16), w_branch_b[0].astype(BF16),
                   w_out[0].astype(BF16), ln1_g[0][None, :], ln1_b[0][None, :])

    pair = lambda a: jnp.concatenate([_chunk_cols(a[..., :d_ff], n_chunks), _chunk_cols(a[..., d_ff:], n_chunks)],
                                     axis=-1)
    w_down_c = w_down[0].astype(BF16).reshape(n_chunks, FF_CHUNK, d)
    return _conv_ffn(x_mid, scale2, shift2, gate2, pair(w_up[0].astype(BF16)), pair(conv_w[0]),
                     pair(conv_b[0][None, :]), w_down_c, ln2_g[0][None, :], ln2_b[0][None, :])
```

```python
import functools

import jax
import jax.numpy as jnp
from jax import lax
from jax.experimental import pallas as pl
from jax.experimental.pallas import tpu as pltpu

F32 = jnp.float32
BF16 = jnp.bfloat16

GRID_W = 64
HEAD_DIM = 64
N_HEADS = 8
N_KV_HEADS = 2
GROUPS = N_HEADS // N_KV_HEADS
WINDOW = 128
ROPE_THETA = 10000.0
LN_EPS = 1e-5
QK_EPS = 1e-6
N_MOD = 6
DEPTH = 1
DEEPNORM_ALPHA = (2.0 * DEPTH) ** 0.25
CONV_WIDTH = 3
LOG2_E = 1.4426950408889634
Q_SCALE = HEAD_DIM ** -0.5 * LOG2_E

LANES = 128
SUBLANES = 8
VMEM_LIMIT_BYTES = 56 * 1024 * 1024

ROW_TILE = 512
FFN_ROW_TILE = 512
MERGE_SPLITS = 2
Q_TILE = 256
K_TILE = 512
PIPE_UNROLL = 4
V_ROWS = HEAD_DIM + 16
FF_CHUNK = 256
HALO = SUBLANES


def _params(n_grid):
    return pltpu.CompilerParams(dimension_semantics=("arbitrary",) * n_grid,
                                vmem_limit_bytes=VMEM_LIMIT_BYTES)


def _resident(shape):
    zeros = (0,) * len(shape)
    return pl.BlockSpec(shape, lambda *_: zeros, pipeline_mode=pl.Buffered(1))


def _mod_kernel(c_ref, w_ref, b_ref, o_ref):
    c = c_ref[...]
    a = (c * jax.nn.sigmoid(c)).astype(BF16)
    o_ref[...] = jnp.dot(a, w_ref[...], preferred_element_type=F32) + b_ref[...]


def _modulation(c_rows, w_mod, b_mod):
    rows, d = c_rows.shape
    n = w_mod.shape[1]
    tn = 1024
    return pl.pallas_call(
        _mod_kernel,
        grid=(n // tn,),
        in_specs=[pl.BlockSpec((rows, d), lambda j: (0, 0)),
                  pl.BlockSpec((d, tn), lambda j: (0, j)),
                  pl.BlockSpec((1, tn), lambda j: (0, j))],
        out_specs=pl.BlockSpec((rows, tn), lambda j: (0, j)),
        out_shape=jax.ShapeDtypeStruct((rows, n), F32),
        compiler_params=_params(1),
        name="modulation",
    )(c_rows, w_mod, b_mod)


def _head_mean_matrix():
    r = lax.broadcasted_iota(jnp.int32, (LANES, LANES), 0) // HEAD_DIM
    c = lax.broadcasted_iota(jnp.int32, (LANES, LANES), 1) // HEAD_DIM
    return jnp.where(r == c, 1.0 / HEAD_DIM, 0.0).astype(BF16)


def _rms_heads(t, gain, gmat):
    ms = jnp.dot((t * t).astype(BF16), gmat, preferred_element_type=F32)
    return t * lax.rsqrt(ms + QK_EPS) * gain


def _rope(t, cos, sin_signed, first_half):
    partner = jnp.where(first_half, pltpu.roll(t, LANES - 16, 1), pltpu.roll(t, 16, 1))
    return t * cos + partner * sin_signed


def _store_vt(vt_ref, v, tile_major):
    vt = v.T.astype(BF16)
    ones = jnp.ones((V_ROWS - HEAD_DIM, v.shape[0]), BF16)
    for kvh in range(N_KV_HEADS):
        blk = vt[kvh * HEAD_DIM:(kvh + 1) * HEAD_DIM]
        if tile_major:
            vt_ref[0, kvh, 0, 0:HEAD_DIM, :] = blk
            vt_ref[0, kvh, 0, HEAD_DIM:V_ROWS, :] = ones
        else:
            vt_ref[0, kvh, 0:HEAD_DIM, :] = blk
            vt_ref[0, kvh, HEAD_DIM:V_ROWS, :] = ones


def _latent_proj_kernel(x_ref, scale_ref, shift_ref, w_ref, b_ref, cos_ref, sin_ref, gq_ref, gk_ref,
                        qa_ref, ka_ref, va_ref, qb_ref, kb_ref, vb_ref, gate_ref, *, offs):
    off_qa, off_ka, off_qb, off_kb, off_g, n_cols = offs
    h = (x_ref[0] * (1.0 + scale_ref[0]) + shift_ref[0]).astype(BF16)
    cos = cos_ref[...]
    sin = sin_ref[...]
    lane = lax.broadcasted_iota(jnp.int32, cos.shape, 1)
    first_half = (lane & 31) < 16
    gmat = _head_mean_matrix()
    q_scale = Q_SCALE

    def proj(lo, n):
        return jnp.dot(h, w_ref[:, lo:lo + n], preferred_element_type=F32) + b_ref[:, lo:lo + n]

    n_q = N_HEADS * HEAD_DIM
    n_gate = n_cols - off_g
    step = 512
    def finish_qb(t):
        for j in range(n_q // LANES):
            s = _rms_heads(t[:, j * LANES:(j + 1) * LANES], gq_ref[...], gmat)
            s = _rope(s, cos, sin, first_half) * q_scale
            qb_ref[0, j * LANES:(j + 1) * LANES, :] = s.T.astype(BF16)

    def finish_qa(t):
        for j in range(n_q // LANES):
            s = _rope(t[:, j * LANES:(j + 1) * LANES], cos, sin, first_half) * q_scale
            qa_ref[0, j * LANES:(j + 1) * LANES, :] = s.T.astype(BF16)

    def finish_kvb(t):
        s = _rms_heads(t[:, :LANES], gk_ref[...], gmat)
        kb_ref[0] = _rope(s, cos, sin, first_half).astype(BF16)
        _store_vt(vb_ref, t[:, LANES:], tile_major=True)

    def finish_kva(t):
        ka_ref[0] = _rope(t[:, :LANES], cos, sin, first_half).astype(BF16)
        _store_vt(va_ref, t[:, LANES:], tile_major=False)

    def finish_gate(j):
        def fn(t):
            gate_ref[0, :, j * step:(j + 1) * step] = jax.nn.sigmoid(t).astype(BF16)
        return fn

    sections = [(off_qb, n_q, finish_qb), (off_qa, n_q, finish_qa),
                (off_kb, 2 * LANES, finish_kvb), (off_ka, 2 * LANES, finish_kva)]
    sections += [(off_g + j * step, step, finish_gate(j)) for j in range(n_gate // step)]
    pending = None
    for lo, n, finish in sections:
        t = proj(lo, n)
        if pending is not None:
            pending[1](pending[0])
        pending = (t, finish)
    pending[1](pending[0])


def _context_proj_kernel(x_ref, scale_ref, shift_ref, w_ref, b_ref, gk_ref,
                         ka_ref, va_ref, kb_ref, vb_ref):
    h = (x_ref[0] * (1.0 + scale_ref[0]) + shift_ref[0]).astype(BF16)
    gmat = _head_mean_matrix()
    t = jnp.dot(h, w_ref[...], preferred_element_type=F32) + b_ref[...]
    ka_ref[0] = t[:, 0:LANES].astype(BF16)
    _store_vt(va_ref, t[:, LANES:2 * LANES], tile_major=False)
    kb_ref[0] = _rms_heads(t[:, 2 * LANES:3 * LANES], gk_ref[...], gmat).astype(BF16)
    _store_vt(vb_ref, t[:, 3 * LANES:4 * LANES], tile_major=False)


def _latent_projection(x, scale, shift, w_in, b_in, cos, sin, gq, gk, offs):
    bsz, n_tok, d = x.shape
    n_cols = w_in.shape[1]
    tm = ROW_TILE
    n_q = N_HEADS * HEAD_DIM
    vec = lambda: pl.BlockSpec((1, 1, d), lambda b, i: (b, 0, 0))
    out_shape = (
        jax.ShapeDtypeStruct((bsz, n_q, n_tok), BF16),
        jax.ShapeDtypeStruct((bsz, n_tok, LANES), BF16),
        jax.ShapeDtypeStruct((bsz, N_KV_HEADS, V_ROWS, n_tok), BF16),
        jax.ShapeDtypeStruct((bsz, n_q, n_tok), BF16),
        jax.ShapeDtypeStruct((bsz, n_tok, LANES), BF16),
        jax.ShapeDtypeStruct((bsz, N_KV_HEADS, n_tok // K_TILE, V_ROWS, K_TILE), BF16),
        jax.ShapeDtypeStruct((bsz, n_tok, n_cols - offs[4]), BF16),
    )
    out_specs = (
        pl.BlockSpec((1, n_q, tm), lambda b, i: (b, 0, i)),
        pl.BlockSpec((1, tm, LANES), lambda b, i: (b, i, 0)),
        pl.BlockSpec((1, N_KV_HEADS, V_ROWS, tm), lambda b, i: (b, 0, 0, i)),
        pl.BlockSpec((1, n_q, tm), lambda b, i: (b, 0, i)),
        pl.BlockSpec((1, tm, LANES), lambda b, i: (b, i, 0)),
        pl.BlockSpec((1, N_KV_HEADS, 1, V_ROWS, K_TILE), lambda b, i: (b, 0, i, 0, 0)),
        pl.BlockSpec((1, tm, n_cols - offs[4]), lambda b, i: (b, i, 0)),
    )
    return pl.pallas_call(
        functools.partial(_latent_proj_kernel, offs=offs + (n_cols,)),
        grid=(bsz, n_tok // tm),
        in_specs=[pl.BlockSpec((1, tm, d), lambda b, i: (b, i, 0)), vec(), vec(),
                  _resident((d, n_cols)), _resident((1, n_cols)),
                  pl.BlockSpec((tm, LANES), lambda b, i: (i, 0)),
                  pl.BlockSpec((tm, LANES), lambda b, i: (i, 0)),
                  _resident((1, LANES)), _resident((1, LANES))],
        out_specs=out_specs,
        out_shape=out_shape,
        compiler_params=_params(2),
        name="latent_projection",
    )(x, scale, shift, w_in, b_in, cos, sin, gq, gk)


def _context_projection(ctx, scale, shift, w_kv, b_kv, gk):
    bsz, n_ctx, d = ctx.shape
    n = w_kv.shape[1]
    vec = lambda: pl.BlockSpec((1, 1, d), lambda b: (0, 0, 0))
    kspec = lambda: pl.BlockSpec((1, n_ctx, LANES), lambda b: (b, 0, 0))
    vspec = lambda: pl.BlockSpec((1, N_KV_HEADS, V_ROWS, n_ctx), lambda b: (b, 0, 0, 0))
    kshape = jax.ShapeDtypeStruct((bsz, n_ctx, LANES), BF16)
    vshape = jax.ShapeDtypeStruct((bsz, N_KV_HEADS, V_ROWS, n_ctx), BF16)
    return pl.pallas_call(
        _context_proj_kernel,
        grid=(bsz,),
        in_specs=[pl.BlockSpec((1, n_ctx, d), lambda b: (b, 0, 0)), vec(), vec(),
                  _resident((d, n)), _resident((1, n)), _resident((1, LANES))],
        out_specs=(kspec(), vspec(), kspec(), vspec()),
        out_shape=(kshape, vshape, kshape, vshape),
        compiler_params=_params(1),
        name="context_projection",
    )(ctx, scale, shift, w_kv, b_kv, gk)


def _expand_queries(qx_ref, qt_ref):
    zeros = jnp.zeros((HEAD_DIM, qt_ref.shape[2]), BF16)
    for hh in range(N_HEADS):
        q = qt_ref[0, hh * HEAD_DIM:(hh + 1) * HEAD_DIM, :]
        if hh // GROUPS == 0:
            qx_ref[hh] = jnp.concatenate([q, zeros], axis=0)
        else:
            qx_ref[hh] = jnp.concatenate([zeros, q], axis=0)


def _window_attn_kernel(sink_ref, qt_ref, kp_ref, kc_ref, kn_ref, kx_ref,
                        vp_ref, vc_ref, vn_ref, vx_ref, o_ref, qx_ref, ot_ref, *, n_tok):
    i = pl.program_id(1)
    tq = qt_ref.shape[2]
    _expand_queries(qx_ref, qt_ref)
    k_all = jnp.concatenate([kp_ref[0], kc_ref[0], kn_ref[0], kx_ref[0]], axis=0)
    n_lat = tq + 2 * WINDOW
    n_keys = k_all.shape[0]
    kk = lax.broadcasted_iota(jnp.int32, (n_keys, tq), 0)
    qq = lax.broadcasted_iota(jnp.int32, (n_keys, tq), 1)
    k_pos = i * tq - WINDOW + kk
    rel = kk - WINDOW - qq
    ok = (kk >= n_lat) | ((jnp.abs(rel) <= WINDOW) & (k_pos >= 0) & (k_pos < n_tok))
    vts = [jnp.concatenate([vp_ref[0, kvh], vc_ref[0, kvh], vn_ref[0, kvh], vx_ref[0, kvh]], axis=1)
           for kvh in range(N_KV_HEADS)]
    ss = [jnp.dot(k_all, qx_ref[hh], preferred_element_type=F32) for hh in range(N_HEADS)]
    for hh in range(N_HEADS):
        s = jnp.where(ok, ss[hh], -jnp.inf)
        sink = sink_ref[hh] * LOG2_E
        m = jnp.maximum(jnp.max(s, axis=0, keepdims=True), sink)
        p = jnp.exp2(s - m).astype(BF16)
        o = jnp.dot(vts[hh // GROUPS], p, preferred_element_type=F32)
        denom = o[HEAD_DIM:HEAD_DIM + 1] + jnp.exp2(sink - m)
        ot_ref[hh * HEAD_DIM:(hh + 1) * HEAD_DIM, :] = o[0:HEAD_DIM] / denom
    o_ref[0] = ot_ref[...].T.astype(BF16)


def _window_attention(sink, qt, k, vt, k_ctx, vt_ctx):
    bsz, n_q, n_tok = qt.shape
    n_ctx = k_ctx.shape[1]
    tq = Q_TILE
    r = tq // WINDOW
    n_wblk = n_tok // WINDOW
    prev = lambda i: jnp.maximum(i * r - 1, 0)
    nxt = lambda i: jnp.minimum((i + 1) * r, n_wblk - 1)
    return pl.pallas_call(
        functools.partial(_window_attn_kernel, n_tok=n_tok),
        grid=(bsz, n_tok // tq),
        in_specs=[pl.BlockSpec(memory_space=pltpu.SMEM),
                  pl.BlockSpec((1, n_q, tq), lambda b, i: (b, 0, i)),
                  pl.BlockSpec((1, WINDOW, LANES), lambda b, i: (b, prev(i), 0)),
                  pl.BlockSpec((1, tq, LANES), lambda b, i: (b, i, 0)),
                  pl.BlockSpec((1, WINDOW, LANES), lambda b, i: (b, nxt(i), 0)),
                  pl.BlockSpec((1, n_ctx, LANES), lambda b, i: (b, 0, 0)),
                  pl.BlockSpec((1, N_KV_HEADS, V_ROWS, WINDOW), lambda b, i: (b, 0, 0, prev(i))),
                  pl.BlockSpec((1, N_KV_HEADS, V_ROWS, tq), lambda b, i: (b, 0, 0, i)),
                  pl.BlockSpec((1, N_KV_HEADS, V_ROWS, WINDOW), lambda b, i: (b, 0, 0, nxt(i))),
                  pl.BlockSpec((1, N_KV_HEADS, V_ROWS, n_ctx), lambda b, i: (b, 0, 0, 0))],
        out_specs=pl.BlockSpec((1, tq, n_q), lambda b, i: (b, i, 0)),
        out_shape=jax.ShapeDtypeStruct((bsz, n_tok, n_q), BF16),
        scratch_shapes=[pltpu.VMEM((N_HEADS, LANES, tq), BF16),
                        pltpu.VMEM((n_q, tq), F32)],
        compiler_params=_params(2),
        name="window_attention",
    )(sink, qt, k, k, k, k_ctx, vt, vt, vt, vt_ctx)


def _global_attn_kernel(qt_ref, k_ref, vt_ref, kx_ref, vx_ref, o_ref,
                        qx_ref, m_ref, acc_ref, ot_ref, s0_ref, s1_ref, mt0_ref, mt1_ref):
    n_tiles = vt_ref.shape[2]
    tk = vt_ref.shape[4]
    _expand_queries(qx_ref, qt_ref)
    m_ref[...] = jnp.full(m_ref.shape, -jnp.inf, F32)
    acc_ref[...] = jnp.zeros(acc_ref.shape, F32)

    def scores(hh, k, s_ref, mt_ref):
        s = jnp.dot(k, qx_ref[hh], preferred_element_type=F32)
        s_ref[hh, 0:k.shape[0], :] = s
        mt_ref[hh] = jnp.max(s, axis=0, keepdims=True)

    def accumulate(hh, vt, n, s_ref, mt_ref):
        m_old = m_ref[hh]
        m_new = jnp.maximum(m_old, mt_ref[hh])
        p = jnp.exp2(s_ref[hh, 0:n, :] - m_new).astype(BF16)
        pv = jnp.dot(vt, p, preferred_element_type=F32)
        acc_ref[hh] = acc_ref[hh] * jnp.exp2(m_old - m_new) + pv
        m_ref[hh] = m_new

    stages = ((s0_ref, mt0_ref), (s1_ref, mt1_ref))
    n_ctx = kx_ref.shape[1]

    def step(i, parity):
        nxt = 0 if i is None else i + 1
        s_nxt, mt_nxt = stages[1 - parity]
        s_cur, mt_cur = stages[parity]
        static = i is None or isinstance(i, int)
        if static and nxt > n_tiles:
            k_nxt = None
        elif static and nxt == n_tiles:
            k_nxt = kx_ref[0]
        elif static:
            k_nxt = k_ref[0, nxt * tk:(nxt + 1) * tk, :]
        else:
            k_nxt = k_ref[0, pl.ds(pl.multiple_of(nxt * tk, tk), tk), :]
        for hh in range(N_HEADS):
            if k_nxt is not None:
                scores(hh, k_nxt, s_nxt, mt_nxt)
            if i is None:
                continue
            if static and i == n_tiles:
                accumulate(hh, vx_ref[0, hh // GROUPS], n_ctx, s_cur, mt_cur)
            else:
                accumulate(hh, vt_ref[0, hh // GROUPS, i], tk, s_cur, mt_cur)

    step(None, 1)
    n_loop = (n_tiles - 1) // PIPE_UNROLL

    def body(j, carry):
        for u in range(PIPE_UNROLL):
            step(PIPE_UNROLL * j + u, u % 2)
        return carry

    lax.fori_loop(0, n_loop, body, 0)
    for i in range(n_loop * PIPE_UNROLL, n_tiles + 1):
        step(i, i % 2)
    for hh in range(N_HEADS):
        acc = acc_ref[hh]
        ot_ref[hh * HEAD_DIM:(hh + 1) * HEAD_DIM, :] = acc[0:HEAD_DIM] / acc[HEAD_DIM:HEAD_DIM + 1]
    o_ref[0] = ot_ref[...].T.astype(BF16)


def _global_attention(qt, k, vt, k_ctx, vt_ctx):
    bsz, n_q, n_tok = qt.shape
    n_ctx = k_ctx.shape[1]
    tq = Q_TILE
    n_tiles, tk = vt.shape[2], vt.shape[4]
    assert n_ctx <= tk and PIPE_UNROLL % 2 == 0
    stage = lambda: [pltpu.VMEM((N_HEADS, tk, tq), F32), pltpu.VMEM((N_HEADS, 1, tq), F32)]
    s0, mt0 = stage()
    s1, mt1 = stage()
    return pl.pallas_call(
        _global_attn_kernel,
        grid=(bsz, n_tok // tq),
        in_specs=[pl.BlockSpec((1, n_q, tq), lambda b, i: (b, 0, i)),
                  pl.BlockSpec((1, n_tok, LANES), lambda b, i: (b, 0, 0)),
                  pl.BlockSpec((1, N_KV_HEADS, n_tiles, V_ROWS, tk), lambda b, i: (b, 0, 0, 0, 0)),
                  pl.BlockSpec((1, n_ctx, LANES), lambda b, i: (b, 0, 0)),
                  pl.BlockSpec((1, N_KV_HEADS, V_ROWS, n_ctx), lambda b, i: (b, 0, 0, 0))],
        out_specs=pl.BlockSpec((1, tq, n_q), lambda b, i: (b, i, 0)),
        out_shape=jax.ShapeDtypeStruct((bsz, n_tok, n_q), BF16),
        scratch_shapes=[pltpu.VMEM((N_HEADS, LANES, tq), BF16),
                        pltpu.VMEM((N_HEADS, 1, tq), F32),
                        pltpu.VMEM((N_HEADS, V_ROWS, tq), F32),
                        pltpu.VMEM((n_q, tq), F32),
                        s0, s1, mt0, mt1],
        compiler_params=_params(2),
        name="global_attention",
    )(qt, k, vt, k_ctx, vt_ctx)


def _layer_norm(z, g, b):
    mu = jnp.mean(z, axis=-1, keepdims=True)
    zc = z - mu
    var = jnp.mean(zc * zc, axis=-1, keepdims=True)
    return zc * lax.rsqrt(var + LN_EPS) * g + b


def _merge_kernel(oa_ref, ob_ref, gate_ref, x_ref, g1_ref, wa_ref, wb_ref, wo_ref, lng_ref, lnb_ref, o_ref):
    tm, d = x_ref.shape[1], x_ref.shape[2]
    n = tm // MERGE_SPLITS
    rows = [slice(r * n, (r + 1) * n) for r in range(MERGE_SPLITS)]
    ab = [(jnp.dot(oa_ref[0, r, :], wa_ref[...], preferred_element_type=F32),
           jnp.dot(ob_ref[0, r, :], wb_ref[...], preferred_element_type=F32)) for r in rows]
    ys = []
    for r, (a, b) in zip(rows, ab):
        g = gate_ref[0, r, :].astype(F32)
        mix = (g[:, :d] * a + g[:, d:] * b).astype(BF16)
        ys.append(jnp.dot(mix, wo_ref[...], preferred_element_type=F32))
    for r, y in zip(rows, ys):
        z = DEEPNORM_ALPHA * x_ref[0, r, :] + g1_ref[0] * y
        o_ref[0, r, :] = _layer_norm(z, lng_ref[...], lnb_ref[...])


def _merge(o_a, o_b, gates, x, gate1, w_a, w_b, w_out, ln_g, ln_b):
    bsz, n_tok, d = x.shape
    n_q = o_a.shape[2]
    tm = ROW_TILE
    row = lambda n: pl.BlockSpec((1, tm, n), lambda b, i: (b, i, 0))
    return pl.pallas_call(
        _merge_kernel,
        grid=(bsz, n_tok // tm),
        in_specs=[row(n_q), row(n_q), row(2 * d), row(d),
                  pl.BlockSpec((1, 1, d), lambda b, i: (b, 0, 0)),
                  _resident((n_q, d)), _resident((n_q, d)), _resident((d, d)),
                  _resident((1, d)), _resident((1, d))],
        out_specs=row(d),
        out_shape=jax.ShapeDtypeStruct((bsz, n_tok, d), F32),
        compiler_params=_params(2),
        name="merge_ln1",
    )(o_a, o_b, gates, x, gate1, w_a, w_b, w_out, ln_g, ln_b)


def _conv_ffn_kernel(x_ref, xp_ref, xn_ref, scale_ref, shift_ref, g2_ref, wup_ref, cw_ref, cb_ref, wdn_ref,
                     lng_ref, lnb_ref, o_ref, acc_ref, h_ref, u0_ref, u1_ref):
    i = pl.program_id(1)
    n_i = pl.num_programs(1)
    tm = x_ref.shape[1]
    n_chunks = wup_ref.shape[0]
    scale = 1.0 + scale_ref[0]
    shift = shift_ref[0]
    hp = jnp.where(i > 0, xp_ref[0] * scale + shift, 0.0)
    hn = jnp.where(i < n_i - 1, xn_ref[0] * scale + shift, 0.0)
    h_ref[...] = jnp.concatenate([hp, x_ref[0] * scale + shift, hn], axis=0).astype(BF16)
    acc_ref[...] = jnp.zeros(acc_ref.shape, F32)

    n_ext = tm + 2 * HALO
    half = tm // 2
    up_rows = ((0, half + 2 * HALO), (half + 2 * HALO, n_ext - half - 2 * HALO))
    down_rows = ((0, half), (half, tm - half))

    def up(c, u_ref, lo, n):
        u_ref[lo:lo + n, :] = jnp.dot(h_ref[lo:lo + n, :], wup_ref[c], preferred_element_type=F32)

    def down(c, u_ref, lo, n):
        w = cw_ref[c]
        r = HALO + lo
        v = (w[0:1] * u_ref[r - 1:r - 1 + n, :] + w[1:2] * u_ref[r:r + n, :] + w[2:3] * u_ref[r + 1:r + 1 + n, :]
             + cb_ref[c])
        gate = v[:, :FF_CHUNK]
        val = v[:, FF_CHUNK:]
        act = (gate * jax.nn.sigmoid(gate) * val).astype(BF16)
        acc_ref[lo:lo + n, :] += jnp.dot(act, wdn_ref[c], preferred_element_type=F32)

    def stage(c_up, u_up, c_down, u_down):
        for (ulo, un), (dlo, dn) in zip(up_rows, down_rows):
            if c_up is not None:
                up(c_up, u_up, ulo, un)
            if c_down is not None:
                down(c_down, u_down, dlo, dn)

    stage(0, u0_ref, None, None)

    def pair(j, carry):
        c = 2 * j
        stage(c + 1, u1_ref, c, u0_ref)
        stage(c + 2, u0_ref, c + 1, u1_ref)
        return carry

    lax.fori_loop(0, (n_chunks - 1) // 2, pair, 0)
    if n_chunks % 2 == 0:
        stage(n_chunks - 1, u1_ref, n_chunks - 2, u0_ref)
        stage(None, None, n_chunks - 1, u1_ref)
    else:
        stage(None, None, n_chunks - 1, u0_ref)
    z = DEEPNORM_ALPHA * x_ref[0] + g2_ref[0] * acc_ref[...]
    o_ref[0] = _layer_norm(z, lng_ref[...], lnb_ref[...])


def _conv_ffn(x, scale, shift, gate2, w_up_c, conv_w_c, conv_b_c, w_down_c, ln_g, ln_b):
    bsz, n_tok, d = x.shape
    tm = FFN_ROW_TILE
    n_chunks = w_up_c.shape[0]
    hb = tm // HALO
    n_hblk = n_tok // HALO
    vec = lambda: pl.BlockSpec((1, 1, d), lambda b, i: (b, 0, 0))
    return pl.pallas_call(
        _conv_ffn_kernel,
        grid=(bsz, n_tok // tm),
        in_specs=[pl.BlockSpec((1, tm, d), lambda b, i: (b, i, 0)),
                  pl.BlockSpec((1, HALO, d), lambda b, i: (b, jnp.maximum(i * hb - 1, 0), 0)),
                  pl.BlockSpec((1, HALO, d), lambda b, i: (b, jnp.minimum((i + 1) * hb, n_hblk - 1), 0)),
                  vec(), vec(), vec(),
                  _resident((n_chunks, d, 2 * FF_CHUNK)),
                  _resident((n_chunks, CONV_WIDTH, 2 * FF_CHUNK)),
                  _resident((n_chunks, 1, 2 * FF_CHUNK)),
                  _resident((n_chunks, FF_CHUNK, d)),
                  _resident((1, d)), _resident((1, d))],
        out_specs=pl.BlockSpec((1, tm, d), lambda b, i: (b, i, 0)),
        out_shape=jax.ShapeDtypeStruct((bsz, n_tok, d), F32),
        scratch_shapes=[pltpu.VMEM((tm, d), F32),
                        pltpu.VMEM((tm + 2 * HALO, d), BF16),
                        pltpu.VMEM((tm + 2 * HALO, 2 * FF_CHUNK), F32),
                        pltpu.VMEM((tm + 2 * HALO, 2 * FF_CHUNK), F32)],
        compiler_params=_params(2),
        name="conv_ffn_ln2",
    )(x, x, x, scale, shift, gate2, w_up_c, conv_w_c, conv_b_c, w_down_c, ln_g, ln_b)


def _rope_tables(n_tok):
    pos = jnp.arange(n_tok, dtype=jnp.int32)
    rows = (pos // GRID_W).astype(F32)
    cols = (pos % GRID_W).astype(F32)
    n_freq = HEAD_DIM // 4
    inv_freq = ROPE_THETA ** (-jnp.arange(n_freq, dtype=F32) / n_freq)
    ang_r = rows[:, None] * inv_freq
    ang_c = cols[:, None] * inv_freq
    cos = jnp.concatenate([jnp.cos(ang_r)] * 2 + [jnp.cos(ang_c)] * 2, axis=-1)
    sin = jnp.concatenate([-jnp.sin(ang_r), jnp.sin(ang_r), -jnp.sin(ang_c), jnp.sin(ang_c)], axis=-1)
    reps = LANES // HEAD_DIM
    return jnp.tile(cos, (1, reps)), jnp.tile(sin, (1, reps))


def _chunk_cols(a, n_chunks):
    lead = a.shape[:-1]
    return jnp.moveaxis(a.reshape(lead + (n_chunks, FF_CHUNK)), -2, 0)


def kernel(x, c, ctx, c_ctx, w_mod, b_mod, w_in, b_in, attn_sink, q_norm_g, k_norm_g, w_branch_a, w_branch_b,
           w_out, ln1_g, ln1_b, w_up, conv_w, conv_b, w_down, ln2_g, ln2_b):
    bsz, n_tok, d = x.shape
    n_q = N_HEADS * HEAD_DIM
    n_kv = N_KV_HEADS * HEAD_DIM
    off_qa = 0
    off_ka = off_qa + n_q
    off_qb = off_ka + 2 * n_kv
    off_kb = off_qb + n_q
    off_g = off_kb + 2 * n_kv
    d_ff = w_down.shape[1]
    n_chunks = d_ff // FF_CHUNK
    assert n_tok % ROW_TILE == 0 and n_tok % K_TILE == 0 and n_tok % GRID_W == 0
    assert d_ff % FF_CHUNK == 0 and bsz <= SUBLANES - 1
    assert w_mod.shape[0] == DEPTH == 1 and ROW_TILE == K_TILE

    cos, sin = _rope_tables(n_tok)
    tile2 = lambda g: jnp.tile(g, LANES // HEAD_DIM)[None, :]
    c_rows = jnp.zeros((SUBLANES, d), F32).at[:bsz].set(c).at[bsz].set(c_ctx)
    mod = _modulation(c_rows, w_mod[0].astype(BF16), b_mod[0][None, :])
    shift1, scale1, gate1, shift2, scale2, gate2 = [mod[:bsz, None, j * d:(j + 1) * d] for j in range(N_MOD)]
    shift_c = mod[bsz:bsz + 1, None, 0:d]
    scale_c = mod[bsz:bsz + 1, None, d:2 * d]

    w_in_l = w_in[0].astype(BF16)
    b_in_l = b_in[0][None, :]
    gq, gk = tile2(q_norm_g[0]), tile2(k_norm_g[0])
    kv_cols = lambda a: jnp.concatenate([a[:, off_ka:off_qb], a[:, off_kb:off_g]], axis=1)
    kc_a, vc_a, kc_b, vc_b = _context_projection(ctx, scale_c, shift_c, kv_cols(w_in_l), kv_cols(b_in_l), gk)
    qa_t, k_a, va_t, qb_t, k_b, vb_t, gates = _latent_projection(
        x, scale1, shift1, w_in_l, b_in_l, cos, sin, gq, gk, (off_qa, off_ka, off_qb, off_kb, off_g))

    o_a = _window_attention(attn_sink[0], qa_t, k_a, va_t, kc_a, vc_a)
    o_b = _global_attention(qb_t, k_b, vb_t, kc_b, vc_b)

    x_mid = _merge(o_a, o_b, gates, x, gate1, w_branch_a[0].astype(BF16), w_branch_b[0].astype(BF16),
                   w_out[0].astype(BF16), ln1_g[0][None, :], ln1_b[0][None, :])

    pair = lambda a: jnp.concatenate([_chunk_cols(a[..., :d_ff], n_chunks), _chunk_cols(a[..., d_ff:], n_chunks)],
                                     axis=-1)
    w_down_c = w_down[0].astype(BF16).reshape(n_chunks, FF_CHUNK, d)
    return _conv_ffn(x_mid, scale2, shift2, gate2, pair(w_up[0].astype(BF16)), pair(conv_w[0]),
                     pair(conv_b[0][None, :]), w_down_c, ln2_g[0][None, :], ln2_b[0][None, :])
```

```python
import functools

import jax
import jax.numpy as jnp
from jax import lax
from jax.experimental import pallas as pl
from jax.experimental.pallas import tpu as pltpu

F32 = jnp.float32
BF16 = jnp.bfloat16

GRID_W = 64
HEAD_DIM = 64
N_HEADS = 8
N_KV_HEADS = 2
GROUPS = N_HEADS // N_KV_HEADS
WINDOW = 128
ROPE_THETA = 10000.0
LN_EPS = 1e-5
QK_EPS = 1e-6
N_MOD = 6
DEPTH = 1
DEEPNORM_ALPHA = (2.0 * DEPTH) ** 0.25
CONV_WIDTH = 3
LOG2_E = 1.4426950408889634
Q_SCALE = HEAD_DIM ** -0.5 * LOG2_E

LANES = 128
SUBLANES = 8
VMEM_LIMIT_BYTES = 56 * 1024 * 1024

ROW_TILE = 512
FFN_ROW_TILE = 512
MERGE_SPLITS = 2
Q_TILE = 256
K_TILE = 512
PIPE_UNROLL = 8
V_ROWS = HEAD_DIM + 16
FF_CHUNK = 256
HALO = SUBLANES


def _params(n_grid):
    return pltpu.CompilerParams(dimension_semantics=("arbitrary",) * n_grid,
                                vmem_limit_bytes=VMEM_LIMIT_BYTES)


def _resident(shape):
    zeros = (0,) * len(shape)
    return pl.BlockSpec(shape, lambda *_: zeros, pipeline_mode=pl.Buffered(1))


def _mod_kernel(c_ref, w_ref, b_ref, o_ref):
    c = c_ref[...]
    a = (c * jax.nn.sigmoid(c)).astype(BF16)
    o_ref[...] = jnp.dot(a, w_ref[...], preferred_element_type=F32) + b_ref[...]


def _modulation(c_rows, w_mod, b_mod):
    rows, d = c_rows.shape
    n = w_mod.shape[1]
    tn = 1024
    return pl.pallas_call(
        _mod_kernel,
        grid=(n // tn,),
        in_specs=[pl.BlockSpec((rows, d), lambda j: (0, 0)),
                  pl.BlockSpec((d, tn), lambda j: (0, j)),
                  pl.BlockSpec((1, tn), lambda j: (0, j))],
        out_specs=pl.BlockSpec((rows, tn), lambda j: (0, j)),
        out_shape=jax.ShapeDtypeStruct((rows, n), F32),
        compiler_params=_params(1),
        name="modulation",
    )(c_rows, w_mod, b_mod)


def _head_mean_matrix():
    r = lax.broadcasted_iota(jnp.int32, (LANES, LANES), 0) // HEAD_DIM
    c = lax.broadcasted_iota(jnp.int32, (LANES, LANES), 1) // HEAD_DIM
    return jnp.where(r == c, 1.0 / HEAD_DIM, 0.0).astype(BF16)


def _rms_heads(t, gain, gmat):
    ms = jnp.dot((t * t).astype(BF16), gmat, preferred_element_type=F32)
    return t * lax.rsqrt(ms + QK_EPS) * gain


def _rope(t, cos, sin_signed, first_half):
    partner = jnp.where(first_half, pltpu.roll(t, LANES - 16, 1), pltpu.roll(t, 16, 1))
    return t * cos + partner * sin_signed


def _store_vt(vt_ref, v, tile_major):
    vt = v.T.astype(BF16)
    ones = jnp.ones((V_ROWS - HEAD_DIM, v.shape[0]), BF16)
    for kvh in range(N_KV_HEADS):
        blk = vt[kvh * HEAD_DIM:(kvh + 1) * HEAD_DIM]
        if tile_major:
            vt_ref[0, kvh, 0, 0:HEAD_DIM, :] = blk
            vt_ref[0, kvh, 0, HEAD_DIM:V_ROWS, :] = ones
        else:
            vt_ref[0, kvh, 0:HEAD_DIM, :] = blk
            vt_ref[0, kvh, HEAD_DIM:V_ROWS, :] = ones


def _latent_proj_kernel(x_ref, scale_ref, shift_ref, w_ref, b_ref, cos_ref, sin_ref, gq_ref, gk_ref,
                        qa_ref, ka_ref, va_ref, qb_ref, kb_ref, vb_ref, gate_ref, *, offs):
    off_qa, off_ka, off_qb, off_kb, off_g, n_cols = offs
    h = (x_ref[0] * (1.0 + scale_ref[0]) + shift_ref[0]).astype(BF16)
    cos = cos_ref[...]
    sin = sin_ref[...]
    lane = lax.broadcasted_iota(jnp.int32, cos.shape, 1)
    first_half = (lane & 31) < 16
    gmat = _head_mean_matrix()
    q_scale = Q_SCALE

    def proj(lo, n):
        return jnp.dot(h, w_ref[:, lo:lo + n], preferred_element_type=F32) + b_ref[:, lo:lo + n]

    n_q = N_HEADS * HEAD_DIM
    n_gate = n_cols - off_g
    step = 512
    def finish_qb(t):
        for j in range(n_q // LANES):
            s = _rms_heads(t[:, j * LANES:(j + 1) * LANES], gq_ref[...], gmat)
            s = _rope(s, cos, sin, first_half) * q_scale
            qb_ref[0, j * LANES:(j + 1) * LANES, :] = s.T.astype(BF16)

    def finish_qa(t):
        for j in range(n_q // LANES):
            s = _rope(t[:, j * LANES:(j + 1) * LANES], cos, sin, first_half) * q_scale
            qa_ref[0, j * LANES:(j + 1) * LANES, :] = s.T.astype(BF16)

    def finish_kvb(t):
        s = _rms_heads(t[:, :LANES], gk_ref[...], gmat)
        kb_ref[0] = _rope(s, cos, sin, first_half).astype(BF16)
        _store_vt(vb_ref, t[:, LANES:], tile_major=True)

    def finish_kva(t):
        ka_ref[0] = _rope(t[:, :LANES], cos, sin, first_half).astype(BF16)
        _store_vt(va_ref, t[:, LANES:], tile_major=False)

    def finish_gate(j):
        def fn(t):
            gate_ref[0, :, j * step:(j + 1) * step] = jax.nn.sigmoid(t).astype(BF16)
        return fn

    sections = [(off_qb, n_q, finish_qb), (off_qa, n_q, finish_qa),
                (off_kb, 2 * LANES, finish_kvb), (off_ka, 2 * LANES, finish_kva)]
    sections += [(off_g + j * step, step, finish_gate(j)) for j in range(n_gate // step)]
    pending = None
    for lo, n, finish in sections:
        t = proj(lo, n)
        if pending is not None:
            pending[1](pending[0])
        pending = (t, finish)
    pending[1](pending[0])


def _context_proj_kernel(x_ref, scale_ref, shift_ref, w_ref, b_ref, gk_ref,
                         ka_ref, va_ref, kb_ref, vb_ref):
    h = (x_ref[0] * (1.0 + scale_ref[0]) + shift_ref[0]).astype(BF16)
    gmat = _head_mean_matrix()
    t = jnp.dot(h, w_ref[...], preferred_element_type=F32) + b_ref[...]
    ka_ref[0] = t[:, 0:LANES].astype(BF16)
    _store_vt(va_ref, t[:, LANES:2 * LANES], tile_major=False)
    kb_ref[0] = _rms_heads(t[:, 2 * LANES:3 * LANES], gk_ref[...], gmat).astype(BF16)
    _store_vt(vb_ref, t[:, 3 * LANES:4 * LANES], tile_major=False)


def _latent_projection(x, scale, shift, w_in, b_in, cos, sin, gq, gk, offs):
    bsz, n_tok, d = x.shape
    n_cols = w_in.shape[1]
    tm = ROW_TILE
    n_q = N_HEADS * HEAD_DIM
    vec = lambda: pl.BlockSpec((1, 1, d), lambda b, i: (b, 0, 0))
    out_shape = (
        jax.ShapeDtypeStruct((bsz, n_q, n_tok), BF16),
        jax.ShapeDtypeStruct((bsz, n_tok, LANES), BF16),
        jax.ShapeDtypeStruct((bsz, N_KV_HEADS, V_ROWS, n_tok), BF16),
        jax.ShapeDtypeStruct((bsz, n_q, n_tok), BF16),
        jax.ShapeDtypeStruct((bsz, n_tok, LANES), BF16),
        jax.ShapeDtypeStruct((bsz, N_KV_HEADS, n_tok // K_TILE, V_ROWS, K_TILE), BF16),
        jax.ShapeDtypeStruct((bsz, n_tok, n_cols - offs[4]), BF16),
    )
    out_specs = (
        pl.BlockSpec((1, n_q, tm), lambda b, i: (b, 0, i)),
        pl.BlockSpec((1, tm, LANES), lambda b, i: (b, i, 0)),
        pl.BlockSpec((1, N_KV_HEADS, V_ROWS, tm), lambda b, i: (b, 0, 0, i)),
        pl.BlockSpec((1, n_q, tm), lambda b, i: (b, 0, i)),
        pl.BlockSpec((1, tm, LANES), lambda b, i: (b, i, 0)),
        pl.BlockSpec((1, N_KV_HEADS, 1, V_ROWS, K_TILE), lambda b, i: (b, 0, i, 0, 0)),
        pl.BlockSpec((1, tm, n_cols - offs[4]), lambda b, i: (b, i, 0)),
    )
    return pl.pallas_call(
        functools.partial(_latent_proj_kernel, offs=offs + (n_cols,)),
        grid=(bsz, n_tok // tm),
        in_specs=[pl.BlockSpec((1, tm, d), lambda b, i: (b, i, 0)), vec(), vec(),
                  _resident((d, n_cols)), _resident((1, n_cols)),
                  pl.BlockSpec((tm, LANES), lambda b, i: (i, 0)),
                  pl.BlockSpec((tm, LANES), lambda b, i: (i, 0)),
                  _resident((1, LANES)), _resident((1, LANES))],
        out_specs=out_specs,
        out_shape=out_shape,
        compiler_params=_params(2),
        name="latent_projection",
    )(x, scale, shift, w_in, b_in, cos, sin, gq, gk)


def _context_projection(ctx, scale, shift, w_kv, b_kv, gk):
    bsz, n_ctx, d = ctx.shape
    n = w_kv.shape[1]
    vec = lambda: pl.BlockSpec((1, 1, d), lambda b: (0, 0, 0))
    kspec = lambda: pl.BlockSpec((1, n_ctx, LANES), lambda b: (b, 0, 0))
    vspec = lambda: pl.BlockSpec((1, N_KV_HEADS, V_ROWS, n_ctx), lambda b: (b, 0, 0, 0))
    kshape = jax.ShapeDtypeStruct((bsz, n_ctx, LANES), BF16)
    vshape = jax.ShapeDtypeStruct((bsz, N_KV_HEADS, V_ROWS, n_ctx), BF16)
    return pl.pallas_call(
        _context_proj_kernel,
        grid=(bsz,),
        in_specs=[pl.BlockSpec((1, n_ctx, d), lambda b: (b, 0, 0)), vec(), vec(),
                  _resident((d, n)), _resident((1, n)), _resident((1, LANES))],
        out_specs=(kspec(), vspec(), kspec(), vspec()),
        out_shape=(kshape, vshape, kshape, vshape),
        compiler_params=_params(1),
        name="context_projection",
    )(ctx, scale, shift, w_kv, b_kv, gk)


def _expand_queries(qx_ref, qt_ref):
    zeros = jnp.zeros((HEAD_DIM, qt_ref.shape[2]), BF16)
    for hh in range(N_HEADS):
        q = qt_ref[0, hh * HEAD_DIM:(hh + 1) * HEAD_DIM, :]
        if hh // GROUPS == 0:
            qx_ref[hh] = jnp.concatenate([q, zeros], axis=0)
        else:
            qx_ref[hh] = jnp.concatenate([zeros, q], axis=0)


def _window_attn_kernel(sink_ref, qt_ref, kp_ref, kc_ref, kn_ref, kx_ref,
                        vp_ref, vc_ref, vn_ref, vx_ref, o_ref, qx_ref, ot_ref, *, n_tok):
    i = pl.program_id(1)
    tq = qt_ref.shape[2]
    _expand_queries(qx_ref, qt_ref)
    k_all = jnp.concatenate([kp_ref[0], kc_ref[0], kn_ref[0], kx_ref[0]], axis=0)
    n_lat = tq + 2 * WINDOW
    n_keys = k_all.shape[0]
    kk = lax.broadcasted_iota(jnp.int32, (n_keys, tq), 0)
    qq = lax.broadcasted_iota(jnp.int32, (n_keys, tq), 1)
    k_pos = i * tq - WINDOW + kk
    rel = kk - WINDOW - qq
    ok = (kk >= n_lat) | ((jnp.abs(rel) <= WINDOW) & (k_pos >= 0) & (k_pos < n_tok))
    vts = [jnp.concatenate([vp_ref[0, kvh], vc_ref[0, kvh], vn_ref[0, kvh], vx_ref[0, kvh]], axis=1)
           for kvh in range(N_KV_HEADS)]
    ss = [jnp.dot(k_all, qx_ref[hh], preferred_element_type=F32) for hh in range(N_HEADS)]
    for hh in range(N_HEADS):
        s = jnp.where(ok, ss[hh], -jnp.inf)
        sink = sink_ref[hh] * LOG2_E
        m = jnp.maximum(jnp.max(s, axis=0, keepdims=True), sink)
        p = jnp.exp2(s - m).astype(BF16)
        o = jnp.dot(vts[hh // GROUPS], p, preferred_element_type=F32)
        denom = o[HEAD_DIM:HEAD_DIM + 1] + jnp.exp2(sink - m)
        ot_ref[hh * HEAD_DIM:(hh + 1) * HEAD_DIM, :] = o[0:HEAD_DIM] / denom
    o_ref[0] = ot_ref[...].T.astype(BF16)


def _window_attention(sink, qt, k, vt, k_ctx, vt_ctx):
    bsz, n_q, n_tok = qt.shape
    n_ctx = k_ctx.shape[1]
    tq = Q_TILE
    r = tq // WINDOW
    n_wblk = n_tok // WINDOW
    prev = lambda i: jnp.maximum(i * r - 1, 0)
    nxt = lambda i: jnp.minimum((i + 1) * r, n_wblk - 1)
    return pl.pallas_call(
        functools.partial(_window_attn_kernel, n_tok=n_tok),
        grid=(bsz, n_tok // tq),
        in_specs=[pl.BlockSpec(memory_space=pltpu.SMEM),
                  pl.BlockSpec((1, n_q, tq), lambda b, i: (b, 0, i)),
                  pl.BlockSpec((1, WINDOW, LANES), lambda b, i: (b, prev(i), 0)),
                  pl.BlockSpec((1, tq, LANES), lambda b, i: (b, i, 0)),
                  pl.BlockSpec((1, WINDOW, LANES), lambda b, i: (b, nxt(i), 0)),
                  pl.BlockSpec((1, n_ctx, LANES), lambda b, i: (b, 0, 0)),
                  pl.BlockSpec((1, N_KV_HEADS, V_ROWS, WINDOW), lambda b, i: (b, 0, 0, prev(i))),
                  pl.BlockSpec((1, N_KV_HEADS, V_ROWS, tq), lambda b, i: (b, 0, 0, i)),
                  pl.BlockSpec((1, N_KV_HEADS, V_ROWS, WINDOW), lambda b, i: (b, 0, 0, nxt(i))),
                  pl.BlockSpec((1, N_KV_HEADS, V_ROWS, n_ctx), lambda b, i: (b, 0, 0, 0))],
        out_specs=pl.BlockSpec((1, tq, n_q), lambda b, i: (b, i, 0)),
        out_shape=jax.ShapeDtypeStruct((bsz, n_tok, n_q), BF16),
        scratch_shapes=[pltpu.VMEM((N_HEADS, LANES, tq), BF16),
                        pltpu.VMEM((n_q, tq), F32)],
        compiler_params=_params(2),
        name="window_attention",
    )(sink, qt, k, k, k, k_ctx, vt, vt, vt, vt_ctx)


def _global_attn_kernel(qt_ref, k_ref, vt_ref, kx_ref, vx_ref, o_ref,
                        qx_ref, m_ref, acc_ref, ot_ref, s0_ref, s1_ref, mt0_ref, mt1_ref):
    n_tiles = vt_ref.shape[2]
    tk = vt_ref.shape[4]
    _expand_queries(qx_ref, qt_ref)
    m_ref[...] = jnp.full(m_ref.shape, -jnp.inf, F32)
    acc_ref[...] = jnp.zeros(acc_ref.shape, F32)

    def scores(hh, k, s_ref, mt_ref):
        s = jnp.dot(k, qx_ref[hh], preferred_element_type=F32)
        s_ref[hh, 0:k.shape[0], :] = s
        mt_ref[hh] = jnp.max(s, axis=0, keepdims=True)

    def accumulate(hh, vt, n, s_ref, mt_ref):
        m_old = m_ref[hh]
        m_new = jnp.maximum(m_old, mt_ref[hh])
        p = jnp.exp2(s_ref[hh, 0:n, :] - m_new).astype(BF16)
        pv = jnp.dot(vt, p, preferred_element_type=F32)
        acc_ref[hh] = acc_ref[hh] * jnp.exp2(m_old - m_new) + pv
        m_ref[hh] = m_new

    stages = ((s0_ref, mt0_ref), (s1_ref, mt1_ref))
    n_ctx = kx_ref.shape[1]

    def step(i, parity):
        nxt = 0 if i is None else i + 1
        s_nxt, mt_nxt = stages[1 - parity]
        s_cur, mt_cur = stages[parity]
        static = i is None or isinstance(i, int)
        if static and nxt > n_tiles:
            k_nxt = None
        elif static and nxt == n_tiles:
            k_nxt = kx_ref[0]
        elif static:
            k_nxt = k_ref[0, nxt * tk:(nxt + 1) * tk, :]
        else:
            k_nxt = k_ref[0, pl.ds(pl.multiple_of(nxt * tk, tk), tk), :]
        for hh in range(N_HEADS):
            if k_nxt is not None:
                scores(hh, k_nxt, s_nxt, mt_nxt)
            if i is None:
                continue
            if static and i == n_tiles:
                accumulate(hh, vx_ref[0, hh // GROUPS], n_ctx, s_cur, mt_cur)
            else:
                accumulate(hh, vt_ref[0, hh // GROUPS, i], tk, s_cur, mt_cur)

    step(None, 1)
    n_loop = (n_tiles - 1) // PIPE_UNROLL

    def body(j, carry):
        for u in range(PIPE_UNROLL):
            step(PIPE_UNROLL * j + u, u % 2)
        return carry

    lax.fori_loop(0, n_loop, body, 0)
    for i in range(n_loop * PIPE_UNROLL, n_tiles + 1):
        step(i, i % 2)
    for hh in range(N_HEADS):
        acc = acc_ref[hh]
        ot_ref[hh * HEAD_DIM:(hh + 1) * HEAD_DIM, :] = acc[0:HEAD_DIM] / acc[HEAD_DIM:HEAD_DIM + 1]
    o_ref[0] = ot_ref[...].T.astype(BF16)


def _global_attention(qt, k, vt, k_ctx, vt_ctx):
    bsz, n_q, n_tok = qt.shape
    n_ctx = k_ctx.shape[1]
    tq = Q_TILE
    n_tiles, tk = vt.shape[2], vt.shape[4]
    assert n_ctx <= tk and PIPE_UNROLL % 2 == 0
    stage = lambda: [pltpu.VMEM((N_HEADS, tk, tq), F32), pltpu.VMEM((N_HEADS, 1, tq), F32)]
    s0, mt0 = stage()
    s1, mt1 = stage()
    return pl.pallas_call(
        _global_attn_kernel,
        grid=(bsz, n_tok // tq),
        in_specs=[pl.BlockSpec((1, n_q, tq), lambda b, i: (b, 0, i)),
                  pl.BlockSpec((1, n_tok, LANES), lambda b, i: (b, 0, 0)),
                  pl.BlockSpec((1, N_KV_HEADS, n_tiles, V_ROWS, tk), lambda b, i: (b, 0, 0, 0, 0)),
                  pl.BlockSpec((1, n_ctx, LANES), lambda b, i: (b, 0, 0)),
                  pl.BlockSpec((1, N_KV_HEADS, V_ROWS, n_ctx), lambda b, i: (b, 0, 0, 0))],
        out_specs=pl.BlockSpec((1, tq, n_q), lambda b, i: (b, i, 0)),
        out_shape=jax.ShapeDtypeStruct((bsz, n_tok, n_q), BF16),
        scratch_shapes=[pltpu.VMEM((N_HEADS, LANES, tq), BF16),
                        pltpu.VMEM((N_HEADS, 1, tq), F32),
                        pltpu.VMEM((N_HEADS, V_ROWS, tq), F32),
                        pltpu.VMEM((n_q, tq), F32),
                        s0, s1, mt0, mt1],
        compiler_params=_params(2),
        name="global_attention",
    )(qt, k, vt, k_ctx, vt_ctx)


def _layer_norm(z, g, b):
    mu = jnp.mean(z, axis=-1, keepdims=True)
    zc = z - mu
    var = jnp.mean(zc * zc, axis=-1, keepdims=True)
    return zc * lax.rsqrt(var + LN_EPS) * g + b


def _merge_kernel(oa_ref, ob_ref, gate_ref, x_ref, g1_ref, wa_ref, wb_ref, wo_ref, lng_ref, lnb_ref, o_ref):
    tm, d = x_ref.shape[1], x_ref.shape[2]
    n = tm // MERGE_SPLITS
    rows = [slice(r * n, (r + 1) * n) for r in range(MERGE_SPLITS)]
    ab = [(jnp.dot(oa_ref[0, r, :], wa_ref[...], preferred_element_type=F32),
           jnp.dot(ob_ref[0, r, :], wb_ref[...], preferred_element_type=F32)) for r in rows]
    ys = []
    for r, (a, b) in zip(rows, ab):
        g = gate_ref[0, r, :].astype(F32)
        mix = (g[:, :d] * a + g[:, d:] * b).astype(BF16)
        ys.append(jnp.dot(mix, wo_ref[...], preferred_element_type=F32))
    for r, y in zip(rows, ys):
        z = DEEPNORM_ALPHA * x_ref[0, r, :] + g1_ref[0] * y
        o_ref[0, r, :] = _layer_norm(z, lng_ref[...], lnb_ref[...])


def _merge(o_a, o_b, gates, x, gate1, w_a, w_b, w_out, ln_g, ln_b):
    bsz, n_tok, d = x.shape
    n_q = o_a.shape[2]
    tm = ROW_TILE
    row = lambda n: pl.BlockSpec((1, tm, n), lambda b, i: (b, i, 0))
    return pl.pallas_call(
        _merge_kernel,
        grid=(bsz, n_tok // tm),
        in_specs=[row(n_q), row(n_q), row(2 * d), row(d),
                  pl.BlockSpec((1, 1, d), lambda b, i: (b, 0, 0)),
                  _resident((n_q, d)), _resident((n_q, d)), _resident((d, d)),
                  _resident((1, d)), _resident((1, d))],
        out_specs=row(d),
        out_shape=jax.ShapeDtypeStruct((bsz, n_tok, d), F32),
        compiler_params=_params(2),
        name="merge_ln1",
    )(o_a, o_b, gates, x, gate1, w_a, w_b, w_out, ln_g, ln_b)


def _conv_ffn_kernel(x_ref, xp_ref, xn_ref, scale_ref, shift_ref, g2_ref, wup_ref, cw_ref, cb_ref, wdn_ref,
                     lng_ref, lnb_ref, o_ref, acc_ref, h_ref, u0_ref, u1_ref):
    i = pl.program_id(1)
    n_i = pl.num_programs(1)
    tm = x_ref.shape[1]
    n_chunks = wdn_ref.shape[0]
    scale = 1.0 + scale_ref[0]
    shift = shift_ref[0]
    hp = jnp.where(i > 0, xp_ref[0] * scale + shift, 0.0)
    hn = jnp.where(i < n_i - 1, xn_ref[0] * scale + shift, 0.0)
    h_ref[...] = jnp.concatenate([hp, x_ref[0] * scale + shift, hn], axis=0).astype(BF16)
    acc_ref[...] = jnp.zeros(acc_ref.shape, F32)

    n_ext = tm + 2 * HALO
    half = tm // 2
    up_rows = ((0, half + 2 * HALO), (half + 2 * HALO, n_ext - half - 2 * HALO))
    down_rows = ((0, half), (half, tm - half))

    def up(c, u_ref, lo, n):
        d_ff = wup_ref.shape[1] // 2
        for part in range(2):
            col = part * d_ff + c * FF_CHUNK
            if not isinstance(col, int):
                col = pl.multiple_of(col, FF_CHUNK)
            u_ref[lo:lo + n, part * FF_CHUNK:(part + 1) * FF_CHUNK] = jnp.dot(
                h_ref[lo:lo + n, :], wup_ref[:, pl.ds(col, FF_CHUNK)], preferred_element_type=F32)

    def down(c, u_ref, lo, n):
        w = cw_ref[c]
        r = HALO + lo
        v = (w[0:1] * u_ref[r - 1:r - 1 + n, :] + w[1:2] * u_ref[r:r + n, :] + w[2:3] * u_ref[r + 1:r + 1 + n, :]
             + cb_ref[c])
        gate = v[:, :FF_CHUNK]
        val = v[:, FF_CHUNK:]
        act = (gate * jax.nn.sigmoid(gate) * val).astype(BF16)
        acc_ref[lo:lo + n, :] += jnp.dot(act, wdn_ref[c], preferred_element_type=F32)

    def stage(c_up, u_up, c_down, u_down):
        for (ulo, un), (dlo, dn) in zip(up_rows, down_rows):
            if c_up is not None:
                up(c_up, u_up, ulo, un)
            if c_down is not None:
                down(c_down, u_down, dlo, dn)

    stage(0, u0_ref, None, None)

    def pair(j, carry):
        c = 2 * j
        stage(c + 1, u1_ref, c, u0_ref)
        stage(c + 2, u0_ref, c + 1, u1_ref)
        return carry

    lax.fori_loop(0, (n_chunks - 1) // 2, pair, 0)
    if n_chunks % 2 == 0:
        stage(n_chunks - 1, u1_ref, n_chunks - 2, u0_ref)
        stage(None, None, n_chunks - 1, u1_ref)
    else:
        stage(None, None, n_chunks - 1, u0_ref)
    z = DEEPNORM_ALPHA * x_ref[0] + g2_ref[0] * acc_ref[...]
    o_ref[0] = _layer_norm(z, lng_ref[...], lnb_ref[...])


def _conv_ffn(x, scale, shift, gate2, w_up, conv_w_c, conv_b_c, w_down_c, ln_g, ln_b):
    bsz, n_tok, d = x.shape
    tm = FFN_ROW_TILE
    n_chunks = w_down_c.shape[0]
    hb = tm // HALO
    n_hblk = n_tok // HALO
    vec = lambda: pl.BlockSpec((1, 1, d), lambda b, i: (b, 0, 0))
    return pl.pallas_call(
        _conv_ffn_kernel,
        grid=(bsz, n_tok // tm),
        in_specs=[pl.BlockSpec((1, tm, d), lambda b, i: (b, i, 0)),
                  pl.BlockSpec((1, HALO, d), lambda b, i: (b, jnp.maximum(i * hb - 1, 0), 0)),
                  pl.BlockSpec((1, HALO, d), lambda b, i: (b, jnp.minimum((i + 1) * hb, n_hblk - 1), 0)),
                  vec(), vec(), vec(),
                  _resident(w_up.shape),
                  _resident((n_chunks, CONV_WIDTH, 2 * FF_CHUNK)),
                  _resident((n_chunks, 1, 2 * FF_CHUNK)),
                  _resident((n_chunks, FF_CHUNK, d)),
                  _resident((1, d)), _resident((1, d))],
        out_specs=pl.BlockSpec((1, tm, d), lambda b, i: (b, i, 0)),
        out_shape=jax.ShapeDtypeStruct((bsz, n_tok, d), F32),
        scratch_shapes=[pltpu.VMEM((tm, d), F32),
                        pltpu.VMEM((tm + 2 * HALO, d), BF16),
                        pltpu.VMEM((tm + 2 * HALO, 2 * FF_CHUNK), F32),
                        pltpu.VMEM((tm + 2 * HALO, 2 * FF_CHUNK), F32)],
        compiler_params=_params(2),
        name="conv_ffn_ln2",
    )(x, x, x, scale, shift, gate2, w_up, conv_w_c, conv_b_c, w_down_c, ln_g, ln_b)


def _rope_tables(n_tok):
    pos = jnp.arange(n_tok, dtype=jnp.int32)
    rows = (pos // GRID_W).astype(F32)
    cols = (pos % GRID_W).astype(F32)
    n_freq = HEAD_DIM // 4
    inv_freq = ROPE_THETA ** (-jnp.arange(n_freq, dtype=F32) / n_freq)
    ang_r = rows[:, None] * inv_freq
    ang_c = cols[:, None] * inv_freq
    cos = jnp.concatenate([jnp.cos(ang_r)] * 2 + [jnp.cos(ang_c)] * 2, axis=-1)
    sin = jnp.concatenate([-jnp.sin(ang_r), jnp.sin(ang_r), -jnp.sin(ang_c), jnp.sin(ang_c)], axis=-1)
    reps = LANES // HEAD_DIM
    return jnp.tile(cos, (1, reps)), jnp.tile(sin, (1, reps))


def _chunk_cols(a, n_chunks):
    lead = a.shape[:-1]
    return jnp.moveaxis(a.reshape(lead + (n_chunks, FF_CHUNK)), -2, 0)


def kernel(x, c, ctx, c_ctx, w_mod, b_mod, w_in, b_in, attn_sink, q_norm_g, k_norm_g, w_branch_a, w_branch_b,
           w_out, ln1_g, ln1_b, w_up, conv_w, conv_b, w_down, ln2_g, ln2_b):
    bsz, n_tok, d = x.shape
    n_q = N_HEADS * HEAD_DIM
    n_kv = N_KV_HEADS * HEAD_DIM
    off_qa = 0
    off_ka = off_qa + n_q
    off_qb = off_ka + 2 * n_kv
    off_kb = off_qb + n_q
    off_g = off_kb + 2 * n_kv
    d_ff = w_down.shape[1]
    n_chunks = d_ff // FF_CHUNK
    assert n_tok % ROW_TILE == 0 and n_tok % K_TILE == 0 and n_tok % GRID_W == 0
    assert d_ff % FF_CHUNK == 0 and bsz <= SUBLANES - 1
    assert w_mod.shape[0] == DEPTH == 1 and ROW_TILE == K_TILE

    cos, sin = _rope_tables(n_tok)
    tile2 = lambda g: jnp.tile(g, LANES // HEAD_DIM)[None, :]
    c_rows = jnp.zeros((SUBLANES, d), F32).at[:bsz].set(c).at[bsz].set(c_ctx)
    mod = _modulation(c_rows, w_mod[0].astype(BF16), b_mod[0][None, :])
    shift1, scale1, gate1, shift2, scale2, gate2 = [mod[:bsz, None, j * d:(j + 1) * d] for j in range(N_MOD)]
    shift_c = mod[bsz:bsz + 1, None, 0:d]
    scale_c = mod[bsz:bsz + 1, None, d:2 * d]

    w_in_l = w_in[0].astype(BF16)
    b_in_l = b_in[0][None, :]
    gq, gk = tile2(q_norm_g[0]), tile2(k_norm_g[0])
    kv_cols = lambda a: jnp.concatenate([a[:, off_ka:off_qb], a[:, off_kb:off_g]], axis=1)
    kc_a, vc_a, kc_b, vc_b = _context_projection(ctx, scale_c, shift_c, kv_cols(w_in_l), kv_cols(b_in_l), gk)
    qa_t, k_a, va_t, qb_t, k_b, vb_t, gates = _latent_projection(
        x, scale1, shift1, w_in_l, b_in_l, cos, sin, gq, gk, (off_qa, off_ka, off_qb, off_kb, off_g))

    o_a = _window_attention(attn_sink[0], qa_t, k_a, va_t, kc_a, vc_a)
    o_b = _global_attention(qb_t, k_b, vb_t, kc_b, vc_b)

    x_mid = _merge(o_a, o_b, gates, x, gate1, w_branch_a[0].astype(BF16), w_branch_b[0].astype(BF16),
                   w_out[0].astype(BF16), ln1_g[0][None, :], ln1_b[0][None, :])

    pair = lambda a: jnp.concatenate([_chunk_cols(a[..., :d_ff], n_chunks), _chunk_cols(a[..., d_ff:], n_chunks)],
                                     axis=-1)
    w_down_c = w_down[0].astype(BF16).reshape(n_chunks, FF_CHUNK, d)
    return _conv_ffn(x_mid, scale2, shift2, gate2, w_up[0].astype(BF16), pair(conv_w[0]),
                     pair(conv_b[0][None, :]), w_down_c, ln2_g[0][None, :], ln2_b[0][None, :])
```

```python
import functools

import jax
import jax.numpy as jnp
from jax import lax
from jax.experimental import pallas as pl
from jax.experimental.pallas import tpu as pltpu

F32 = jnp.float32
BF16 = jnp.bfloat16

GRID_W = 64
HEAD_DIM = 64
N_HEADS = 8
N_KV_HEADS = 2
GROUPS = N_HEADS // N_KV_HEADS
WINDOW = 128
ROPE_THETA = 10000.0
LN_EPS = 1e-5
QK_EPS = 1e-6
N_MOD = 6
DEPTH = 1
DEEPNORM_ALPHA = (2.0 * DEPTH) ** 0.25
CONV_WIDTH = 3
LOG2_E = 1.4426950408889634
Q_SCALE = HEAD_DIM ** -0.5 * LOG2_E

LANES = 128
SUBLANES = 8
VMEM_LIMIT_BYTES = 56 * 1024 * 1024

ROW_TILE = 512
FFN_ROW_TILE = 512
MERGE_ROW_TILE = 1024
MERGE_SPLITS = 4
Q_TILE = 256
GLOBAL_Q_TILE = 256
K_TILE = 512
PIPE_UNROLL = 8
V_ROWS = HEAD_DIM + 16
FF_CHUNK = 256
HALO = SUBLANES


def _params(n_grid):
    return pltpu.CompilerParams(dimension_semantics=("arbitrary",) * n_grid,
                                vmem_limit_bytes=VMEM_LIMIT_BYTES)


def _resident(shape):
    zeros = (0,) * len(shape)
    return pl.BlockSpec(shape, lambda *_: zeros, pipeline_mode=pl.Buffered(1))


def _mod_kernel(c_ref, w_ref, b_ref, o_ref):
    c = c_ref[...]
    a = (c * jax.nn.sigmoid(c)).astype(BF16)
    o_ref[...] = jnp.dot(a, w_ref[...], preferred_element_type=F32) + b_ref[...]


def _modulation(c_rows, w_mod, b_mod):
    rows, d = c_rows.shape
    n = w_mod.shape[1]
    tn = 1024
    return pl.pallas_call(
        _mod_kernel,
        grid=(n // tn,),
        in_specs=[pl.BlockSpec((rows, d), lambda j: (0, 0)),
                  pl.BlockSpec((d, tn), lambda j: (0, j)),
                  pl.BlockSpec((1, tn), lambda j: (0, j))],
        out_specs=pl.BlockSpec((rows, tn), lambda j: (0, j)),
        out_shape=jax.ShapeDtypeStruct((rows, n), F32),
        compiler_params=_params(1),
        name="modulation",
    )(c_rows, w_mod, b_mod)


def _head_mean_matrix():
    r = lax.broadcasted_iota(jnp.int32, (LANES, LANES), 0) // HEAD_DIM
    c = lax.broadcasted_iota(jnp.int32, (LANES, LANES), 1) // HEAD_DIM
    return jnp.where(r == c, 1.0 / HEAD_DIM, 0.0).astype(BF16)


def _rms_heads(t, gain, gmat):
    ms = jnp.dot((t * t).astype(BF16), gmat, preferred_element_type=F32)
    return t * lax.rsqrt(ms + QK_EPS) * gain


def _rope(t, cos, sin_signed, first_half):
    partner = jnp.where(first_half, pltpu.roll(t, LANES - 16, 1), pltpu.roll(t, 16, 1))
    return t * cos + partner * sin_signed


def _store_vt(vt_ref, v, tile_major):
    vt = v.T.astype(BF16)
    ones = jnp.ones((V_ROWS - HEAD_DIM, v.shape[0]), BF16)
    for kvh in range(N_KV_HEADS):
        blk = vt[kvh * HEAD_DIM:(kvh + 1) * HEAD_DIM]
        if tile_major:
            vt_ref[0, kvh, 0, 0:HEAD_DIM, :] = blk
            vt_ref[0, kvh, 0, HEAD_DIM:V_ROWS, :] = ones
        else:
            vt_ref[0, kvh, 0:HEAD_DIM, :] = blk
            vt_ref[0, kvh, HEAD_DIM:V_ROWS, :] = ones


def _latent_proj_kernel(x_ref, scale_ref, shift_ref, w_ref, b_ref, cos_ref, sin_ref, gq_ref, gk_ref,
                        qa_ref, ka_ref, va_ref, qb_ref, kb_ref, vb_ref, gate_ref, *, offs):
    off_qa, off_ka, off_qb, off_kb, off_g, n_cols = offs
    h = (x_ref[0] * (1.0 + scale_ref[0]) + shift_ref[0]).astype(BF16)
    cos = cos_ref[...]
    sin = sin_ref[...]
    lane = lax.broadcasted_iota(jnp.int32, cos.shape, 1)
    first_half = (lane & 31) < 16
    gmat = _head_mean_matrix()
    q_scale = Q_SCALE

    def proj(lo, n):
        return jnp.dot(h, w_ref[:, lo:lo + n], preferred_element_type=F32) + b_ref[:, lo:lo + n]

    n_q = N_HEADS * HEAD_DIM
    n_gate = n_cols - off_g
    step = 512
    def finish_qb(t):
        for j in range(n_q // LANES):
            s = _rms_heads(t[:, j * LANES:(j + 1) * LANES], gq_ref[...], gmat)
            s = _rope(s, cos, sin, first_half) * q_scale
            qb_ref[0, j * LANES:(j + 1) * LANES, :] = s.T.astype(BF16)

    def finish_qa(t):
        for j in range(n_q // LANES):
            s = _rope(t[:, j * LANES:(j + 1) * LANES], cos, sin, first_half) * q_scale
            qa_ref[0, j * LANES:(j + 1) * LANES, :] = s.T.astype(BF16)

    def finish_kvb(t):
        s = _rms_heads(t[:, :LANES], gk_ref[...], gmat)
        kb_ref[0] = _rope(s, cos, sin, first_half).astype(BF16)
        _store_vt(vb_ref, t[:, LANES:], tile_major=True)

    def finish_kva(t):
        ka_ref[0] = _rope(t[:, :LANES], cos, sin, first_half).astype(BF16)
        _store_vt(va_ref, t[:, LANES:], tile_major=False)

    def finish_gate(j):
        def fn(t):
            gate_ref[0, :, j * step:(j + 1) * step] = jax.nn.sigmoid(t).astype(BF16)
        return fn

    sections = [(off_qb, n_q, finish_qb), (off_qa, n_q, finish_qa),
                (off_kb, 2 * LANES, finish_kvb), (off_ka, 2 * LANES, finish_kva)]
    sections += [(off_g + j * step, step, finish_gate(j)) for j in range(n_gate // step)]
    pending = None
    for lo, n, finish in sections:
        t = proj(lo, n)
        if pending is not None:
            pending[1](pending[0])
        pending = (t, finish)
    pending[1](pending[0])


def _context_proj_kernel(x_ref, scale_ref, shift_ref, w_ref, b_ref, gk_ref,
                         ka_ref, va_ref, kb_ref, vb_ref):
    h = (x_ref[0] * (1.0 + scale_ref[0]) + shift_ref[0]).astype(BF16)
    gmat = _head_mean_matrix()
    t = jnp.dot(h, w_ref[...], preferred_element_type=F32) + b_ref[...]
    ka_ref[0] = t[:, 0:LANES].astype(BF16)
    _store_vt(va_ref, t[:, LANES:2 * LANES], tile_major=False)
    kb_ref[0] = _rms_heads(t[:, 2 * LANES:3 * LANES], gk_ref[...], gmat).astype(BF16)
    _store_vt(vb_ref, t[:, 3 * LANES:4 * LANES], tile_major=False)


def _latent_projection(x, scale, shift, w_in, b_in, cos, sin, gq, gk, offs):
    bsz, n_tok, d = x.shape
    n_cols = w_in.shape[1]
    tm = ROW_TILE
    n_q = N_HEADS * HEAD_DIM
    vec = lambda: pl.BlockSpec((1, 1, d), lambda b, i: (b, 0, 0))
    out_shape = (
        jax.ShapeDtypeStruct((bsz, n_q, n_tok), BF16),
        jax.ShapeDtypeStruct((bsz, n_tok, LANES), BF16),
        jax.ShapeDtypeStruct((bsz, N_KV_HEADS, V_ROWS, n_tok), BF16),
        jax.ShapeDtypeStruct((bsz, n_q, n_tok), BF16),
        jax.ShapeDtypeStruct((bsz, n_tok, LANES), BF16),
        jax.ShapeDtypeStruct((bsz, N_KV_HEADS, n_tok // K_TILE, V_ROWS, K_TILE), BF16),
        jax.ShapeDtypeStruct((bsz, n_tok, n_cols - offs[4]), BF16),
    )
    out_specs = (
        pl.BlockSpec((1, n_q, tm), lambda b, i: (b, 0, i)),
        pl.BlockSpec((1, tm, LANES), lambda b, i: (b, i, 0)),
        pl.BlockSpec((1, N_KV_HEADS, V_ROWS, tm), lambda b, i: (b, 0, 0, i)),
        pl.BlockSpec((1, n_q, tm), lambda b, i: (b, 0, i)),
        pl.BlockSpec((1, tm, LANES), lambda b, i: (b, i, 0)),
        pl.BlockSpec((1, N_KV_HEADS, 1, V_ROWS, K_TILE), lambda b, i: (b, 0, i, 0, 0)),
        pl.BlockSpec((1, tm, n_cols - offs[4]), lambda b, i: (b, i, 0)),
    )
    return pl.pallas_call(
        functools.partial(_latent_proj_kernel, offs=offs + (n_cols,)),
        grid=(bsz, n_tok // tm),
        in_specs=[pl.BlockSpec((1, tm, d), lambda b, i: (b, i, 0)), vec(), vec(),
                  _resident((d, n_cols)), _resident((1, n_cols)),
                  pl.BlockSpec((tm, LANES), lambda b, i: (i, 0)),
                  pl.BlockSpec((tm, LANES), lambda b, i: (i, 0)),
                  _resident((1, LANES)), _resident((1, LANES))],
        out_specs=out_specs,
        out_shape=out_shape,
        compiler_params=_params(2),
        name="latent_projection",
    )(x, scale, shift, w_in, b_in, cos, sin, gq, gk)


def _context_projection(ctx, scale, shift, w_kv, b_kv, gk):
    bsz, n_ctx, d = ctx.shape
    n = w_kv.shape[1]
    vec = lambda: pl.BlockSpec((1, 1, d), lambda b: (0, 0, 0))
    kspec = lambda: pl.BlockSpec((1, n_ctx, LANES), lambda b: (b, 0, 0))
    vspec = lambda: pl.BlockSpec((1, N_KV_HEADS, V_ROWS, n_ctx), lambda b: (b, 0, 0, 0))
    kshape = jax.ShapeDtypeStruct((bsz, n_ctx, LANES), BF16)
    vshape = jax.ShapeDtypeStruct((bsz, N_KV_HEADS, V_ROWS, n_ctx), BF16)
    return pl.pallas_call(
        _context_proj_kernel,
        grid=(bsz,),
        in_specs=[pl.BlockSpec((1, n_ctx, d), lambda b: (b, 0, 0)), vec(), vec(),
                  _resident((d, n)), _resident((1, n)), _resident((1, LANES))],
        out_specs=(kspec(), vspec(), kspec(), vspec()),
        out_shape=(kshape, vshape, kshape, vshape),
        compiler_params=_params(1),
        name="context_projection",
    )(ctx, scale, shift, w_kv, b_kv, gk)


def _expand_queries(qx_ref, qt_ref):
    zeros = jnp.zeros((HEAD_DIM, qt_ref.shape[2]), BF16)
    for hh in range(N_HEADS):
        q = qt_ref[0, hh * HEAD_DIM:(hh + 1) * HEAD_DIM, :]
        if hh // GROUPS == 0:
            qx_ref[hh] = jnp.concatenate([q, zeros], axis=0)
        else:
            qx_ref[hh] = jnp.concatenate([zeros, q], axis=0)


def _window_attn_kernel(sink_ref, qt_ref, kp_ref, kc_ref, kn_ref, kx_ref,
                        vp_ref, vc_ref, vn_ref, vx_ref, o_ref, qx_ref, ot_ref):
    i = pl.program_id(1)
    tq = qt_ref.shape[2]
    w = WINDOW
    _expand_queries(qx_ref, qt_ref)
    k_all = jnp.concatenate([kp_ref[0], kc_ref[0], kn_ref[0], kx_ref[0]], axis=0)
    n_lat_blocks = (tq + 2 * w) // w
    n_key_blocks = k_all.shape[0] // w
    n_q_blocks = tq // w
    r_idx = lax.broadcasted_iota(jnp.int32, (w, w), 0)
    c_idx = lax.broadcasted_iota(jnp.int32, (w, w), 1)
    in_seq = {0: i > 0, n_lat_blocks - 1: i < pl.num_programs(1) - 1}
    tri = {-1: r_idx >= c_idx, 1: r_idx <= c_idx}

    def block_mask(kb, qb):
        if kb >= n_lat_blocks:
            return None
        d = kb - 1 - qb
        if d == 0:
            return None
        if d not in tri:
            return False
        return tri[d] & in_seq[kb] if kb in in_seq else tri[d]

    vts = [jnp.concatenate([vp_ref[0, kvh], vc_ref[0, kvh], vn_ref[0, kvh], vx_ref[0, kvh]], axis=1)
           for kvh in range(N_KV_HEADS)]
    ss = [jnp.dot(k_all, qx_ref[hh], preferred_element_type=F32) for hh in range(N_HEADS)]
    for hh in range(N_HEADS):
        sink = sink_ref[hh] * LOG2_E
        p_cols, m_cols = [], []
        for qb in range(n_q_blocks):
            blocks = []
            for kb in range(n_key_blocks):
                mask = block_mask(kb, qb)
                if mask is False:
                    blocks.append(None)
                    continue
                s = ss[hh][kb * w:(kb + 1) * w, qb * w:(qb + 1) * w]
                blocks.append(s if mask is None else jnp.where(mask, s, -jnp.inf))
            m = sink
            for s in blocks:
                if s is not None:
                    m = jnp.maximum(m, jnp.max(s, axis=0, keepdims=True))
            m_cols.append(m)
            p_cols.append(jnp.concatenate(
                [jnp.zeros((w, w), BF16) if s is None else jnp.exp2(s - m).astype(BF16) for s in blocks], axis=0))
        p = jnp.concatenate(p_cols, axis=1)
        m = jnp.concatenate(m_cols, axis=1)
        o = jnp.dot(vts[hh // GROUPS], p, preferred_element_type=F32)
        denom = o[HEAD_DIM:HEAD_DIM + 1] + jnp.exp2(sink - m)
        ot_ref[hh * HEAD_DIM:(hh + 1) * HEAD_DIM, :] = o[0:HEAD_DIM] / denom
    o_ref[0] = ot_ref[...].T.astype(BF16)


def _window_attention(sink, qt, k, vt, k_ctx, vt_ctx):
    bsz, n_q, n_tok = qt.shape
    n_ctx = k_ctx.shape[1]
    tq = Q_TILE
    r = tq // WINDOW
    n_wblk = n_tok // WINDOW
    prev = lambda i: jnp.maximum(i * r - 1, 0)
    nxt = lambda i: jnp.minimum((i + 1) * r, n_wblk - 1)
    return pl.pallas_call(
        _window_attn_kernel,
        grid=(bsz, n_tok // tq),
        in_specs=[pl.BlockSpec(memory_space=pltpu.SMEM),
                  pl.BlockSpec((1, n_q, tq), lambda b, i: (b, 0, i)),
                  pl.BlockSpec((1, WINDOW, LANES), lambda b, i: (b, prev(i), 0)),
                  pl.BlockSpec((1, tq, LANES), lambda b, i: (b, i, 0)),
                  pl.BlockSpec((1, WINDOW, LANES), lambda b, i: (b, nxt(i), 0)),
                  pl.BlockSpec((1, n_ctx, LANES), lambda b, i: (b, 0, 0)),
                  pl.BlockSpec((1, N_KV_HEADS, V_ROWS, WINDOW), lambda b, i: (b, 0, 0, prev(i))),
                  pl.BlockSpec((1, N_KV_HEADS, V_ROWS, tq), lambda b, i: (b, 0, 0, i)),
                  pl.BlockSpec((1, N_KV_HEADS, V_ROWS, WINDOW), lambda b, i: (b, 0, 0, nxt(i))),
                  pl.BlockSpec((1, N_KV_HEADS, V_ROWS, n_ctx), lambda b, i: (b, 0, 0, 0))],
        out_specs=pl.BlockSpec((1, tq, n_q), lambda b, i: (b, i, 0)),
        out_shape=jax.ShapeDtypeStruct((bsz, n_tok, n_q), BF16),
        scratch_shapes=[pltpu.VMEM((N_HEADS, LANES, tq), BF16),
                        pltpu.VMEM((n_q, tq), F32)],
        compiler_params=_params(2),
        name="window_attention",
    )(sink, qt, k, k, k, k_ctx, vt, vt, vt, vt_ctx)


def _global_attn_kernel(qt_ref, k_ref, vt_ref, kx_ref, vx_ref, o_ref,
                        qx_ref, m_ref, acc_ref, ot_ref, s0_ref, s1_ref, mt0_ref, mt1_ref):
    n_tiles = vt_ref.shape[2]
    tk = vt_ref.shape[4]
    _expand_queries(qx_ref, qt_ref)
    m_ref[...] = jnp.full(m_ref.shape, -jnp.inf, F32)
    acc_ref[...] = jnp.zeros(acc_ref.shape, F32)

    def scores(hh, k, s_ref, mt_ref):
        s = jnp.dot(k, qx_ref[hh], preferred_element_type=F32)
        s_ref[hh, 0:k.shape[0], :] = s
        mt_ref[hh] = jnp.max(s, axis=0, keepdims=True)

    def accumulate(hh, vt, n, s_ref, mt_ref):
        m_old = m_ref[hh]
        m_new = jnp.maximum(m_old, mt_ref[hh])
        p = jnp.exp2(s_ref[hh, 0:n, :] - m_new).astype(BF16)
        pv = jnp.dot(vt, p, preferred_element_type=F32)
        acc_ref[hh] = acc_ref[hh] * jnp.exp2(m_old - m_new) + pv
        m_ref[hh] = m_new

    stages = ((s0_ref, mt0_ref), (s1_ref, mt1_ref))
    n_ctx = kx_ref.shape[1]

    def step(i, parity):
        nxt = 0 if i is None else i + 1
        s_nxt, mt_nxt = stages[1 - parity]
        s_cur, mt_cur = stages[parity]
        static = i is None or isinstance(i, int)
        if static and nxt > n_tiles:
            k_nxt = None
        elif static and nxt == n_tiles:
            k_nxt = kx_ref[0]
        elif static:
            k_nxt = k_ref[0, nxt * tk:(nxt + 1) * tk, :]
        else:
            k_nxt = k_ref[0, pl.ds(pl.multiple_of(nxt * tk, tk), tk), :]
        for hh in range(N_HEADS):
            if k_nxt is not None:
                scores(hh, k_nxt, s_nxt, mt_nxt)
            if i is None:
                continue
            if static and i == n_tiles:
                accumulate(hh, vx_ref[0, hh // GROUPS], n_ctx, s_cur, mt_cur)
            else:
                accumulate(hh, vt_ref[0, hh // GROUPS, i], tk, s_cur, mt_cur)

    step(None, 1)
    n_loop = (n_tiles - 1) // PIPE_UNROLL

    def body(j, carry):
        for u in range(PIPE_UNROLL):
            step(PIPE_UNROLL * j + u, u % 2)
        return carry

    lax.fori_loop(0, n_loop, body, 0)
    for i in range(n_loop * PIPE_UNROLL, n_tiles + 1):
        step(i, i % 2)
    for hh in range(N_HEADS):
        acc = acc_ref[hh]
        ot_ref[hh * HEAD_DIM:(hh + 1) * HEAD_DIM, :] = acc[0:HEAD_DIM] / acc[HEAD_DIM:HEAD_DIM + 1]
    o_ref[0] = ot_ref[...].T.astype(BF16)


def _global_attention(qt, k, vt, k_ctx, vt_ctx):
    bsz, n_q, n_tok = qt.shape
    n_ctx = k_ctx.shape[1]
    tq = GLOBAL_Q_TILE
    n_tiles, tk = vt.shape[2], vt.shape[4]
    assert n_ctx <= tk and PIPE_UNROLL % 2 == 0
    stage = lambda: [pltpu.VMEM((N_HEADS, tk, tq), F32), pltpu.VMEM((N_HEADS, 1, tq), F32)]
    s0, mt0 = stage()
    s1, mt1 = stage()
    return pl.pallas_call(
        _global_attn_kernel,
        grid=(bsz, n_tok // tq),
        in_specs=[pl.BlockSpec((1, n_q, tq), lambda b, i: (b, 0, i)),
                  pl.BlockSpec((1, n_tok, LANES), lambda b, i: (b, 0, 0)),
                  pl.BlockSpec((1, N_KV_HEADS, n_tiles, V_ROWS, tk), lambda b, i: (b, 0, 0, 0, 0)),
                  pl.BlockSpec((1, n_ctx, LANES), lambda b, i: (b, 0, 0)),
                  pl.BlockSpec((1, N_KV_HEADS, V_ROWS, n_ctx), lambda b, i: (b, 0, 0, 0))],
        out_specs=pl.BlockSpec((1, tq, n_q), lambda b, i: (b, i, 0)),
        out_shape=jax.ShapeDtypeStruct((bsz, n_tok, n_q), BF16),
        scratch_shapes=[pltpu.VMEM((N_HEADS, LANES, tq), BF16),
                        pltpu.VMEM((N_HEADS, 1, tq), F32),
                        pltpu.VMEM((N_HEADS, V_ROWS, tq), F32),
                        pltpu.VMEM((n_q, tq), F32),
                        s0, s1, mt0, mt1],
        compiler_params=_params(2),
        name="global_attention",
    )(qt, k, vt, k_ctx, vt_ctx)


def _layer_norm(z, g, b):
    mu = jnp.mean(z, axis=-1, keepdims=True)
    zc = z - mu
    var = jnp.mean(zc * zc, axis=-1, keepdims=True)
    return zc * lax.rsqrt(var + LN_EPS) * g + b


def _merge_kernel(oa_ref, ob_ref, gate_ref, x_ref, g1_ref, wa_ref, wb_ref, wo_ref, lng_ref, lnb_ref, o_ref):
    tm, d = x_ref.shape[1], x_ref.shape[2]
    n = tm // MERGE_SPLITS
    rows = [slice(r * n, (r + 1) * n) for r in range(MERGE_SPLITS)]
    ab = [(jnp.dot(oa_ref[0, r, :], wa_ref[...], preferred_element_type=F32),
           jnp.dot(ob_ref[0, r, :], wb_ref[...], preferred_element_type=F32)) for r in rows]
    ys = []
    for r, (a, b) in zip(rows, ab):
        g = gate_ref[0, r, :].astype(F32)
        mix = (g[:, :d] * a + g[:, d:] * b).astype(BF16)
        ys.append(jnp.dot(mix, wo_ref[...], preferred_element_type=F32))
    for r, y in zip(rows, ys):
        z = DEEPNORM_ALPHA * x_ref[0, r, :] + g1_ref[0] * y
        o_ref[0, r, :] = _layer_norm(z, lng_ref[...], lnb_ref[...])


def _merge(o_a, o_b, gates, x, gate1, w_a, w_b, w_out, ln_g, ln_b):
    bsz, n_tok, d = x.shape
    n_q = o_a.shape[2]
    tm = MERGE_ROW_TILE
    row = lambda n: pl.BlockSpec((1, tm, n), lambda b, i: (b, i, 0))
    return pl.pallas_call(
        _merge_kernel,
        grid=(bsz, n_tok // tm),
        in_specs=[row(n_q), row(n_q), row(2 * d), row(d),
                  pl.BlockSpec((1, 1, d), lambda b, i: (b, 0, 0)),
                  _resident((n_q, d)), _resident((n_q, d)), _resident((d, d)),
                  _resident((1, d)), _resident((1, d))],
        out_specs=row(d),
        out_shape=jax.ShapeDtypeStruct((bsz, n_tok, d), F32),
        compiler_params=_params(2),
        name="merge_ln1",
    )(o_a, o_b, gates, x, gate1, w_a, w_b, w_out, ln_g, ln_b)


def _conv_ffn_kernel(x_ref, xp_ref, xn_ref, scale_ref, shift_ref, g2_ref, wup_ref, cw_ref, cb_ref, wdn_ref,
                     lng_ref, lnb_ref, o_ref, acc_ref, h_ref, u0_ref, u1_ref):
    i = pl.program_id(1)
    n_i = pl.num_programs(1)
    tm = x_ref.shape[1]
    n_chunks = wdn_ref.shape[0]
    scale = 1.0 + scale_ref[0]
    shift = shift_ref[0]
    hp = jnp.where(i > 0, xp_ref[0] * scale + shift, 0.0)
    hn = jnp.where(i < n_i - 1, xn_ref[0] * scale + shift, 0.0)
    h_ref[...] = jnp.concatenate([hp, x_ref[0] * scale + shift, hn], axis=0).astype(BF16)
    acc_ref[...] = jnp.zeros(acc_ref.shape, F32)

    n_ext = tm + 2 * HALO
    half = tm // 2
    up_rows = ((0, half + 2 * HALO), (half + 2 * HALO, n_ext - half - 2 * HALO))
    down_rows = ((0, half), (half, tm - half))

    def up(c, u_ref, lo, n):
        d_ff = wup_ref.shape[1] // 2
        for part in range(2):
            col = part * d_ff + c * FF_CHUNK
            if not isinstance(col, int):
                col = pl.multiple_of(col, FF_CHUNK)
            u_ref[lo:lo + n, part * FF_CHUNK:(part + 1) * FF_CHUNK] = jnp.dot(
                h_ref[lo:lo + n, :], wup_ref[:, pl.ds(col, FF_CHUNK)], preferred_element_type=F32)

    def down(c, u_ref, lo, n):
        w = cw_ref[c]
        r = HALO + lo
        v = (w[0:1] * u_ref[r - 1:r - 1 + n, :] + w[1:2] * u_ref[r:r + n, :] + w[2:3] * u_ref[r + 1:r + 1 + n, :]
             + cb_ref[c])
        gate = v[:, :FF_CHUNK]
        val = v[:, FF_CHUNK:]
        act = (gate * jax.nn.sigmoid(gate) * val).astype(BF16)
        acc_ref[lo:lo + n, :] += jnp.dot(act, wdn_ref[c], preferred_element_type=F32)

    def stage(c_up, u_up, c_down, u_down):
        for (ulo, un), (dlo, dn) in zip(up_rows, down_rows):
            if c_up is not None:
                up(c_up, u_up, ulo, un)
            if c_down is not None:
                down(c_down, u_down, dlo, dn)

    stage(0, u0_ref, None, None)

    def pair(j, carry):
        c = 2 * j
        stage(c + 1, u1_ref, c, u0_ref)
        stage(c + 2, u0_ref, c + 1, u1_ref)
        return carry

    lax.fori_loop(0, (n_chunks - 1) // 2, pair, 0)
    if n_chunks % 2 == 0:
        stage(n_chunks - 1, u1_ref, n_chunks - 2, u0_ref)
        stage(None, None, n_chunks - 1, u1_ref)
    else:
        stage(None, None, n_chunks - 1, u0_ref)
    z = DEEPNORM_ALPHA * x_ref[0] + g2_ref[0] * acc_ref[...]
    o_ref[0] = _layer_norm(z, lng_ref[...], lnb_ref[...])


def _conv_ffn(x, scale, shift, gate2, w_up, conv_w_c, conv_b_c, w_down_c, ln_g, ln_b):
    bsz, n_tok, d = x.shape
    tm = FFN_ROW_TILE
    n_chunks = w_down_c.shape[0]
    hb = tm // HALO
    n_hblk = n_tok // HALO
    vec = lambda: pl.BlockSpec((1, 1, d), lambda b, i: (b, 0, 0))
    return pl.pallas_call(
        _conv_ffn_kernel,
        grid=(bsz, n_tok // tm),
        in_specs=[pl.BlockSpec((1, tm, d), lambda b, i: (b, i, 0)),
                  pl.BlockSpec((1, HALO, d), lambda b, i: (b, jnp.maximum(i * hb - 1, 0), 0)),
                  pl.BlockSpec((1, HALO, d), lambda b, i: (b, jnp.minimum((i + 1) * hb, n_hblk - 1), 0)),
                  vec(), vec(), vec(),
                  _resident(w_up.shape),
                  _resident((n_chunks, CONV_WIDTH, 2 * FF_CHUNK)),
                  _resident((n_chunks, 1, 2 * FF_CHUNK)),
                  _resident((n_chunks, FF_CHUNK, d)),
                  _resident((1, d)), _resident((1, d))],
        out_specs=pl.BlockSpec((1, tm, d), lambda b, i: (b, i, 0)),
        out_shape=jax.ShapeDtypeStruct((bsz, n_tok, d), F32),
        scratch_shapes=[pltpu.VMEM((tm, d), F32),
                        pltpu.VMEM((tm + 2 * HALO, d), BF16),
                        pltpu.VMEM((tm + 2 * HALO, 2 * FF_CHUNK), F32),
                        pltpu.VMEM((tm + 2 * HALO, 2 * FF_CHUNK), F32)],
        compiler_params=_params(2),
        name="conv_ffn_ln2",
    )(x, x, x, scale, shift, gate2, w_up, conv_w_c, conv_b_c, w_down_c, ln_g, ln_b)


def _rope_tables(n_tok):
    pos = jnp.arange(n_tok, dtype=jnp.int32)
    rows = (pos // GRID_W).astype(F32)
    cols = (pos % GRID_W).astype(F32)
    n_freq = HEAD_DIM // 4
    inv_freq = ROPE_THETA ** (-jnp.arange(n_freq, dtype=F32) / n_freq)
    ang_r = rows[:, None] * inv_freq
    ang_c = cols[:, None] * inv_freq
    cos = jnp.concatenate([jnp.cos(ang_r)] * 2 + [jnp.cos(ang_c)] * 2, axis=-1)
    sin = jnp.concatenate([-jnp.sin(ang_r), jnp.sin(ang_r), -jnp.sin(ang_c), jnp.sin(ang_c)], axis=-1)
    reps = LANES // HEAD_DIM
    return jnp.tile(cos, (1, reps)), jnp.tile(sin, (1, reps))


def _chunk_cols(a, n_chunks):
    lead = a.shape[:-1]
    return jnp.moveaxis(a.reshape(lead + (n_chunks, FF_CHUNK)), -2, 0)


def kernel(x, c, ctx, c_ctx, w_mod, b_mod, w_in, b_in, attn_sink, q_norm_g, k_norm_g, w_branch_a, w_branch_b,
           w_out, ln1_g, ln1_b, w_up, conv_w, conv_b, w_down, ln2_g, ln2_b):
    bsz, n_tok, d = x.shape
    n_q = N_HEADS * HEAD_DIM
    n_kv = N_KV_HEADS * HEAD_DIM
    off_qa = 0
    off_ka = off_qa + n_q
    off_qb = off_ka + 2 * n_kv
    off_kb = off_qb + n_q
    off_g = off_kb + 2 * n_kv
    d_ff = w_down.shape[1]
    n_chunks = d_ff // FF_CHUNK
    assert n_tok % ROW_TILE == 0 and n_tok % K_TILE == 0 and n_tok % GRID_W == 0
    assert n_tok % MERGE_ROW_TILE == 0 and n_tok % FFN_ROW_TILE == 0 and n_tok % Q_TILE == 0
    assert d_ff % FF_CHUNK == 0 and bsz <= SUBLANES - 1
    assert w_mod.shape[0] == DEPTH == 1 and ROW_TILE == K_TILE

    cos, sin = _rope_tables(n_tok)
    tile2 = lambda g: jnp.tile(g, LANES // HEAD_DIM)[None, :]
    c_rows = jnp.zeros((SUBLANES, d), F32).at[:bsz].set(c).at[bsz].set(c_ctx)
    mod = _modulation(c_rows, w_mod[0].astype(BF16), b_mod[0][None, :])
    shift1, scale1, gate1, shift2, scale2, gate2 = [mod[:bsz, None, j * d:(j + 1) * d] for j in range(N_MOD)]
    shift_c = mod[bsz:bsz + 1, None, 0:d]
    scale_c = mod[bsz:bsz + 1, None, d:2 * d]

    w_in_l = w_in[0].astype(BF16)
    b_in_l = b_in[0][None, :]
    gq, gk = tile2(q_norm_g[0]), tile2(k_norm_g[0])
    kv_cols = lambda a: jnp.concatenate([a[:, off_ka:off_qb], a[:, off_kb:off_g]], axis=1)
    kc_a, vc_a, kc_b, vc_b = _context_projection(ctx, scale_c, shift_c, kv_cols(w_in_l), kv_cols(b_in_l), gk)
    qa_t, k_a, va_t, qb_t, k_b, vb_t, gates = _latent_projection(
        x, scale1, shift1, w_in_l, b_in_l, cos, sin, gq, gk, (off_qa, off_ka, off_qb, off_kb, off_g))

    o_a = _window_attention(attn_sink[0], qa_t, k_a, va_t, kc_a, vc_a)
    o_b = _global_attention(qb_t, k_b, vb_t, kc_b, vc_b)

    x_mid = _merge(o_a, o_b, gates, x, gate1, w_branch_a[0].astype(BF16), w_branch_b[0].astype(BF16),
                   w_out[0].astype(BF16), ln1_g[0][None, :], ln1_b[0][None, :])

    pair = lambda a: jnp.concatenate([_chunk_cols(a[..., :d_ff], n_chunks), _chunk_cols(a[..., d_ff:], n_chunks)],
                                     axis=-1)
    w_down_c = w_down[0].astype(BF16).reshape(n_chunks, FF_CHUNK, d)
    return _conv_ffn(x_mid, scale2, shift2, gate2, w_up[0].astype(BF16), pair(conv_w[0]),
                     pair(conv_b[0][None, :]), w_down_c, ln2_g[0][None, :], ln2_b[0][None, :])
```
